```python
import math
import jax, jax.numpy as jnp
from jax import lax
import numpy as np

D_MODEL = 2048
BATCH = 1
SEQ = 16384
DEPTH = 2

N_MIXERS = 2
N_HEADS = 16
HEAD_DIM = D_MODEL // N_HEADS
MOBA_BLOCK = 256
MOBA_TOPK = 3
Q_CHUNK = 64
CONV_WIDTH = 31
N_GROUPS = 4
EXPERTS_PER_GROUP = 8
TOP_K_EXPERTS = 2
D_EXPERT = D_MODEL // 4
LN_EPS = 1e-5
NEG_INF = -1e30
DEEPNORM_ALPHA = (2.0 * DEPTH) ** 0.25
DEEPNORM_BETA = (8.0 * DEPTH) ** -0.25
N_ATTN_LAYERS = (DEPTH + N_MIXERS - 1) // N_MIXERS
N_CONV_LAYERS = DEPTH // N_MIXERS

kernel_name = "hybrid_moba_conformer_hmoe_deepnorm"


def layer_norm(x, g, b):
    xf = x.astype(jnp.float32)
    mu = jnp.mean(xf, axis=-1, keepdims=True)
    var = jnp.mean(jnp.square(xf - mu), axis=-1, keepdims=True)
    y = (xf - mu) * lax.rsqrt(var + LN_EPS) * g.astype(jnp.float32) + b.astype(jnp.float32)
    return y.astype(x.dtype)


def alibi_slopes():
    return 2.0 ** (-8.0 * jnp.arange(1, N_HEADS + 1, dtype=jnp.float32) / N_HEADS)


def moba_attention(x, w_qkv, w_o):
    B, S, D = x.shape
    qkv = x @ w_qkv
    q, k, v = jnp.split(qkv, 3, axis=-1)
    q = q.reshape(B, S, N_HEADS, HEAD_DIM)
    k = k.reshape(B, S, N_HEADS, HEAD_DIM)
    v = v.reshape(B, S, N_HEADS, HEAD_DIM)
    n_blocks = -(-S // MOBA_BLOCK)
    s_pad = n_blocks * MOBA_BLOCK
    k_eff = min(MOBA_TOPK, n_blocks)
    pad = ((0, 0), (0, s_pad - S), (0, 0), (0, 0))
    q_p, k_p, v_p = jnp.pad(q, pad), jnp.pad(k, pad), jnp.pad(v, pad)
    kb = k_p.reshape(B, n_blocks, MOBA_BLOCK, N_HEADS, HEAD_DIM).transpose(0, 3, 1, 2, 4)
    vb = v_p.reshape(B, n_blocks, MOBA_BLOCK, N_HEADS, HEAD_DIM).transpose(0, 3, 1, 2, 4)
    k_mean = jnp.mean(kb.astype(jnp.float32), axis=3)
    slopes = alibi_slopes()
    scale = HEAD_DIM ** -0.5
    blk_ids = jnp.arange(n_blocks)
    b_i = jnp.arange(B)[:, None, None, None]
    h_i = jnp.arange(N_HEADS)[None, None, :, None]
    n_chunks = s_pad // Q_CHUNK

    def chunk_fn(c):
        t0 = c * Q_CHUNK
        own = t0 // MOBA_BLOCK
        t = (t0 + jnp.arange(Q_CHUNK)).astype(jnp.float32)
        qc = lax.dynamic_slice_in_dim(q_p, t0, Q_CHUNK, axis=1)
        gate = jnp.einsum('bqhd,bhnd->bqhn', qc.astype(jnp.float32), k_mean)
        gate = jnp.where((blk_ids < own)[None, None, None, :], gate, NEG_INF)
        _, top_i = lax.top_k(gate, k_eff)
        valid = jnp.arange(k_eff) < own
        kg = kb[b_i, h_i, top_i]
        vg = vb[b_i, h_i, top_i]
        s_sel = jnp.einsum('bqhd,bqhkjd->bqhkj', qc, kg).astype(jnp.float32) * scale
        key_pos = (top_i[..., None] * MOBA_BLOCK + jnp.arange(MOBA_BLOCK)).astype(jnp.float32)
        dist_sel = t[None, :, None, None, None] - key_pos
        s_sel = s_sel - slopes[None, None, :, None, None] * dist_sel
        s_sel = jnp.where(valid[None, None, None, :, None], s_sel, NEG_INF)
        ko = lax.dynamic_slice_in_dim(k_p, own * MOBA_BLOCK, MOBA_BLOCK, axis=1)
        vo = lax.dynamic_slice_in_dim(v_p, own * MOBA_BLOCK, MOBA_BLOCK, axis=1)
        s_own = jnp.einsum('bqhd,bkhd->bqhk', qc, ko).astype(jnp.float32) * scale
        own_pos = (own * MOBA_BLOCK + jnp.arange(MOBA_BLOCK)).astype(jnp.float32)
        dist_own = t[:, None] - own_pos[None, :]
        s_own = s_own - slopes[None, None, :, None] * dist_own[None, :, None, :]
        s_own = jnp.where((dist_own >= 0)[None, :, None, :], s_own, NEG_INF)
        scores = jnp.concatenate(
            [s_sel.reshape(B, Q_CHUNK, N_HEADS, k_eff * MOBA_BLOCK), s_own], axis=-1)
        p = jax.nn.softmax(scores, axis=-1).astype(v.dtype)
        p_sel = p[..., :k_eff * MOBA_BLOCK].reshape(B, Q_CHUNK, N_HEADS, k_eff, MOBA_BLOCK)
        p_own = p[..., k_eff * MOBA_BLOCK:]
        return (jnp.einsum('bqhkj,bqhkjd->bqhd', p_sel, vg)
                + jnp.einsum('bqhk,bkhd->bqhd', p_own, vo))

    outs = lax.map(chunk_fn, jnp.arange(n_chunks))
    o = jnp.moveaxis(outs, 0, 1).reshape(B, s_pad, D)[:, :S]
    return o @ w_o


def conformer_conv(x, w_pw1, b_pw1, w_dw, b_dw, ln_g, ln_b, w_pw2):
    D = x.shape[-1]
    h = x @ w_pw1 + b_pw1
    h = h[..., :D] * jax.nn.sigmoid(h[..., D:])
    h = lax.conv_general_dilated(
        h, w_dw.reshape(CONV_WIDTH, 1, D),
        window_strides=(1,), padding=[(CONV_WIDTH - 1, 0)],
        dimension_numbers=('NWC', 'WIO', 'NWC'),
        feature_group_count=D) + b_dw
    h = jax.nn.silu(layer_norm(h, ln_g, ln_b))
    return h @ w_pw2


def hierarchical_moe(x, w_grp, b_grp, w_rt, b_rt, w_gate, w_up, w_down):
    B, S, D = x.shape
    xt = x.reshape(B * S, D)
    g_logits = (xt @ w_grp + b_grp).astype(jnp.float32)
    g_prob = jax.nn.softmax(g_logits, axis=-1)
    g_sel = jnp.argmax(g_logits, axis=-1)
    g_p = jnp.take_along_axis(g_prob, g_sel[:, None], axis=1)
    e_logits = (jnp.einsum('td,dge->tge', xt, w_rt) + b_rt).astype(jnp.float32)
    e_logits = jnp.take_along_axis(e_logits, g_sel[:, None, None], axis=1)[:, 0]
    top_v, top_i = lax.top_k(e_logits, TOP_K_EXPERTS)
    w2 = jax.nn.softmax(top_v, axis=-1) * g_p
    onehot_g = jax.nn.one_hot(g_sel, N_GROUPS, dtype=jnp.float32)
    e_gate = jnp.einsum('tk,tke->te', w2, jax.nn.one_hot(top_i, EXPERTS_PER_GROUP, dtype=jnp.float32))
    comb = (onehot_g[:, :, None] * e_gate[:, None, :]).astype(x.dtype)
    y = jnp.zeros_like(xt)
    for g in range(N_GROUPS):
        h = jax.nn.silu(jnp.einsum('td,edf->tef', xt, w_gate[g])) * jnp.einsum('td,edf->tef', xt, w_up[g])
        h = h * comb[:, g, :, None]
        y = y + jnp.einsum('tef,efd->td', h, w_down[g])
    return y.reshape(B, S, D)


def setup_inputs(seed: int = 0) -> dict:
    key = jax.random.key(seed)
    ks = jax.random.split(key, 24)
    D, F, G, E = D_MODEL, D_EXPERT, N_GROUPS, EXPERTS_PER_GROUP
    f32 = jnp.float32
    nrm = lambda k, shape, s: jax.random.normal(k, shape, f32) * s
    ones_n = lambda k, shape: 1.0 + 0.02 * jax.random.normal(k, shape, f32)
    qkv_scale = jnp.concatenate([jnp.ones((2 * D,), f32), jnp.full((D,), DEEPNORM_BETA, f32)])
    return {
        "x": jax.random.normal(ks[0], (BATCH, SEQ, D), f32),
        "attn_w_qkv": nrm(ks[1], (N_ATTN_LAYERS, D, 3 * D), D ** -0.5) * qkv_scale,
        "attn_w_o": nrm(ks[2], (N_ATTN_LAYERS, D, D), DEEPNORM_BETA * D ** -0.5),
        "conv_w_pw1": nrm(ks[3], (N_CONV_LAYERS, D, 2 * D), D ** -0.5),
        "conv_b_pw1": nrm(ks[4], (N_CONV_LAYERS, 2 * D), 0.02),
        "conv_w_dw": nrm(ks[5], (N_CONV_LAYERS, CONV_WIDTH, D), CONV_WIDTH ** -0.5),
        "conv_b_dw": nrm(ks[6], (N_CONV_LAYERS, D), 0.02),
        "conv_ln_g": ones_n(ks[7], (N_CONV_LAYERS, D)),
        "conv_ln_b": nrm(ks[8], (N_CONV_LAYERS, D), 0.02),
        "conv_w_pw2": nrm(ks[9], (N_CONV_LAYERS, D, D), DEEPNORM_BETA * D ** -0.5),
        "mix_ln_g": ones_n(ks[10], (DEPTH, D)),
        "mix_ln_b": nrm(ks[11], (DEPTH, D), 0.02),
        "moe_w_grp": nrm(ks[12], (DEPTH, D, G), D ** -0.5),
        "moe_b_grp": nrm(ks[13], (DEPTH, G), 0.01),
        "moe_w_rt": nrm(ks[14], (DEPTH, D, G, E), D ** -0.5),
        "moe_b_rt": nrm(ks[15], (DEPTH, G, E), 0.01),
        "moe_w_gate": nrm(ks[16], (DEPTH, G, E, D, F), D ** -0.5),
        "moe_w_up": nrm(ks[17], (DEPTH, G, E, D, F), D ** -0.5),
        "moe_w_down": nrm(ks[18], (DEPTH, G, E, F, D), DEEPNORM_BETA * F ** -0.5),
        "ffn_ln_g": ones_n(ks[19], (DEPTH, D)),
        "ffn_ln_b": nrm(ks[20], (DEPTH, D), 0.02),
    }


def reference(x, attn_w_qkv, attn_w_o, conv_w_pw1, conv_b_pw1, conv_w_dw, conv_b_dw,
              conv_ln_g, conv_ln_b, conv_w_pw2, mix_ln_g, mix_ln_b, moe_w_grp, moe_b_grp,
              moe_w_rt, moe_b_rt, moe_w_gate, moe_w_up, moe_w_down, ffn_ln_g, ffn_ln_b):
    for i in range(DEPTH):
        j = i // N_MIXERS
        if i % N_MIXERS == 0:
            y = moba_attention(x, attn_w_qkv[j], attn_w_o[j])
        else:
            y = conformer_conv(x, conv_w_pw1[j], conv_b_pw1[j], conv_w_dw[j], conv_b_dw[j],
                               conv_ln_g[j], conv_ln_b[j], conv_w_pw2[j])
        x = layer_norm(DEEPNORM_ALPHA * x + y, mix_ln_g[i], mix_ln_b[i])
        y = hierarchical_moe(x, moe_w_grp[i], moe_b_grp[i], moe_w_rt[i], moe_b_rt[i],
                             moe_w_gate[i], moe_w_up[i], moe_w_down[i])
        x = layer_norm(DEEPNORM_ALPHA * x + y, ffn_ln_g[i], ffn_ln_b[i])
    return x
```

```python
import functools

import jax
import jax.numpy as jnp
from jax import lax
from jax.experimental import pallas as pl
from jax.experimental.pallas import tpu as pltpu

F32 = jnp.float32
BF16 = jnp.bfloat16

D_MODEL = 2048
SEQ = 16384
DEPTH = 2
N_HEADS = 16
HEAD_DIM = D_MODEL // N_HEADS
MOBA_BLOCK = 256
MOBA_TOPK = 3
N_BLOCKS = SEQ // MOBA_BLOCK
CONV_WIDTH = 31
N_GROUPS = 4
EXPERTS_PER_GROUP = 8
N_EXPERTS = N_GROUPS * EXPERTS_PER_GROUP
D_EXPERT = D_MODEL // 4
LN_EPS = 1e-5
NEG_INF = -1e30
DEEPNORM_ALPHA = (2.0 * DEPTH) ** 0.25

LANES = 128
CONV_HALO = 32
TM_ROUTE = 256
TM_MOVE = 256
TM_EXPERT = 256
N_EXPERT_TILES = (2 * SEQ) // TM_EXPERT + N_EXPERTS
N_SORTED_ROWS = N_EXPERT_TILES * TM_EXPERT
VMEM_LIMIT = 56 * 1024 * 1024


def _round_to_bf16(v):
    return lax.reduce_precision(v, exponent_bits=8, mantissa_bits=7)


def _cparams(sem):
    return pltpu.CompilerParams(dimension_semantics=sem, vmem_limit_bytes=VMEM_LIMIT)


def _qkv_kernel(x_ref, w_ref, o_ref, *, n_scaled_tiles, scale):
    acc = jnp.dot(x_ref[...].astype(BF16), w_ref[...], preferred_element_type=F32)
    s = jnp.where(pl.program_id(1) < n_scaled_tiles, scale, 1.0).astype(F32)
    o_ref[...] = (acc * s).astype(o_ref.dtype)


def _qkv_proj(x, w_bf16):
    tm, tn = 1024, 1024
    t, d = x.shape
    n = w_bf16.shape[1]
    kern = functools.partial(_qkv_kernel, n_scaled_tiles=D_MODEL // tn, scale=HEAD_DIM ** -0.5)
    return pl.pallas_call(
        kern,
        out_shape=jax.ShapeDtypeStruct((t, n), BF16),
        grid=(t // tm, n // tn),
        in_specs=[pl.BlockSpec((tm, d), lambda i, j: (i, 0)),
                  pl.BlockSpec((d, tn), lambda i, j: (0, j))],
        out_specs=pl.BlockSpec((tm, tn), lambda i, j: (i, j)),
        compiler_params=_cparams(("parallel", "arbitrary")),
        name="qkv_proj",
    )(x, w_bf16)


def _glu_kernel(x_ref, wa_ref, wb_ref, ba_ref, bb_ref, o_ref):
    xb = x_ref[...].astype(BF16)
    a = jnp.dot(xb, wa_ref[...], preferred_element_type=F32) + ba_ref[...]
    b = jnp.dot(xb, wb_ref[...], preferred_element_type=F32) + bb_ref[...]
    o_ref[...] = a * jax.nn.sigmoid(b)


def _pw1_glu(x, w_bf16, bias):
    tm, tn = 1024, 512
    t, d = x.shape
    nb = d // tn
    bias2 = bias.reshape(1, 2 * d)
    return pl.pallas_call(
        _glu_kernel,
        out_shape=jax.ShapeDtypeStruct((t, d), F32),
        grid=(t // tm, nb),
        in_specs=[pl.BlockSpec((tm, d), lambda i, j: (i, 0)),
                  pl.BlockSpec((d, tn), lambda i, j: (0, j)),
                  pl.BlockSpec((d, tn), lambda i, j: (0, j + nb)),
                  pl.BlockSpec((1, tn), lambda i, j: (0, j)),
                  pl.BlockSpec((1, tn), lambda i, j: (0, j + nb))],
        out_specs=pl.BlockSpec((tm, tn), lambda i, j: (i, j)),
        compiler_params=_cparams(("parallel", "arbitrary")),
        name="pw1_glu",
    )(x, w_bf16, w_bf16, bias2, bias2)


N_BIAS_SPLIT = 3


def _moba_key_table():
    slopes = 2.0 ** (-8.0 * jnp.arange(1, N_HEADS + 1, dtype=F32) / N_HEADS)
    kpos = jnp.arange(SEQ, dtype=F32)
    bias = slopes[:, None] * kpos[None, :]
    pieces = []
    rem = bias
    for _ in range(N_BIAS_SPLIT):
        p = _round_to_bf16(rem)
        pieces.append(p.astype(BF16))
        rem = rem - p
    onehot = (jnp.arange(SEQ)[:, None] // MOBA_BLOCK == jnp.arange(N_BLOCKS)[None, :]).astype(BF16)
    onehot = jnp.broadcast_to(onehot[None], (N_HEADS, SEQ, N_BLOCKS))
    pad = jnp.zeros((N_HEADS, SEQ, LANES - N_BLOCKS - N_BIAS_SPLIT), BF16)
    return jnp.concatenate([onehot] + [p[..., None] for p in pieces] + [pad], axis=-1)


def _moba_kernel(q_ref, k_ref, v_ref, c_ref, o_ref, kmean_ref, m_ref, l_ref, acc_ref):
    blk = MOBA_BLOCK
    own = pl.program_id(1)

    @pl.when(own == 0)
    def _():
        kmean_ref[...] = jnp.zeros_like(kmean_ref)

        def body(j, carry):
            kb = k_ref[pl.ds(pl.multiple_of(j * blk, blk), blk), :].astype(F32)
            kmean_ref[pl.ds(j, 1), :] = jnp.mean(kb, axis=0, keepdims=True)
            return carry

        lax.fori_loop(0, N_BLOCKS, body, 0)

    q = q_ref[...]
    km = kmean_ref[...]
    km_hi = km.astype(BF16)
    km_lo = (km - km_hi.astype(F32)).astype(BF16)
    nt = (((1,), (1,)), ((), ()))
    gate = (lax.dot_general(q, km_hi, nt, preferred_element_type=F32)
            + lax.dot_general(q, km_lo, nt, preferred_element_type=F32))
    lane = lax.broadcasted_iota(jnp.int32, gate.shape, 1)
    minus_inf = jnp.float32(-jnp.inf)
    gm = jnp.where(lane < own, gate, minus_inf)
    sel = lane == own
    for r in range(MOBA_TOPK):
        mx = jnp.max(gm, axis=1, keepdims=True)
        idx = jnp.min(jnp.where(gm == mx, lane, LANES), axis=1, keepdims=True)
        hit = lane == idx
        sel = sel | (hit & (r < own))
        gm = jnp.where(hit, minus_inf, gm)
    aug = jnp.where(lane < N_BLOCKS, jnp.where(sel, 0.0, NEG_INF),
                    jnp.where(lane < N_BLOCKS + N_BIAS_SPLIT, 1.0, 0.0)).astype(BF16)
    q_aug = jnp.concatenate([q, aug], axis=1)

    def scores(j):
        rows = pl.ds(pl.multiple_of(j * blk, blk), blk)
        k_aug = jnp.concatenate([k_ref[rows, :], c_ref[rows, :]], axis=1)
        return lax.dot_general(q_aug, k_aug, nt, preferred_element_type=F32), rows

    s, rows = scores(own)
    r_i = lax.broadcasted_iota(jnp.int32, s.shape, 0)
    c_i = lax.broadcasted_iota(jnp.int32, s.shape, 1)
    s = jnp.where(r_i >= c_i, s, NEG_INF)
    m0 = jnp.max(s, axis=1, keepdims=True)
    p = jnp.exp(s - m0)
    m_ref[...] = m0
    l_ref[...] = jnp.sum(p, axis=1, keepdims=True)
    acc_ref[...] = jnp.dot(p.astype(BF16), v_ref[rows, :], preferred_element_type=F32)

    def body(j, carry):
        s, rows = scores(j)
        m_old = m_ref[...]
        m_new = jnp.maximum(m_old, jnp.max(s, axis=1, keepdims=True))
        alpha = jnp.exp(m_old - m_new)
        p = jnp.exp(s - m_new)
        l_ref[...] = alpha * l_ref[...] + jnp.sum(p, axis=1, keepdims=True)
        acc_ref[...] = alpha * acc_ref[...] + jnp.dot(p.astype(BF16), v_ref[rows, :],
                                                      preferred_element_type=F32)
        m_ref[...] = m_new
        return carry

    lax.fori_loop(0, own, body, 0)
    o_ref[...] = (acc_ref[...] / l_ref[...]).astype(o_ref.dtype)


def _moba_attention(qkv, key_table):
    blk = MOBA_BLOCK
    return pl.pallas_call(
        _moba_kernel,
        out_shape=jax.ShapeDtypeStruct((SEQ, D_MODEL), BF16),
        grid=(N_HEADS, N_BLOCKS),
        in_specs=[pl.BlockSpec((blk, HEAD_DIM), lambda h, i: (i, h)),
                  pl.BlockSpec((SEQ, HEAD_DIM), lambda h, i: (0, N_HEADS + h)),
                  pl.BlockSpec((SEQ, HEAD_DIM), lambda h, i: (0, 2 * N_HEADS + h)),
                  pl.BlockSpec((None, SEQ, LANES), lambda h, i: (h, 0, 0))],
        out_specs=pl.BlockSpec((blk, HEAD_DIM), lambda h, i: (i, h)),
        scratch_shapes=[pltpu.VMEM((LANES, HEAD_DIM), F32),
                        pltpu.VMEM((blk, 1), F32),
                        pltpu.VMEM((blk, 1), F32),
                        pltpu.VMEM((blk, HEAD_DIM), F32)],
        compiler_params=_cparams(("arbitrary", "arbitrary")),
        name="moba_attention",
    )(qkv, qkv, qkv, key_table)


def _dwconv_kernel(cur_ref, prev_ref, w_ref, b_ref, o_ref, win_ref, *, row_chunk):
    tm = cur_ref.shape[0]
    keep = (pl.program_id(0) > 0).astype(F32)
    win_ref[pl.ds(0, CONV_HALO), :] = prev_ref[...] * keep
    win_ref[pl.ds(CONV_HALO, tm), :] = cur_ref[...]
    first = CONV_HALO - (CONV_WIDTH - 1)
    for rc in range(tm // row_chunk):
        acc = jnp.broadcast_to(b_ref[...], (row_chunk, cur_ref.shape[1]))
        for k in range(CONV_WIDTH):
            acc = acc + w_ref[pl.ds(k, 1), :] * win_ref[pl.ds(first + k + rc * row_chunk, row_chunk), :]
        o_ref[pl.ds(rc * row_chunk, row_chunk), :] = acc


def _dwconv(g, w_dw, b_dw):
    tm, tc = 512, 512
    t, d = g.shape
    w_pad = jnp.concatenate([w_dw, jnp.zeros((CONV_HALO - CONV_WIDTH, d), F32)], axis=0)
    halo_per_tile = tm // CONV_HALO
    kern = functools.partial(_dwconv_kernel, row_chunk=64)
    return pl.pallas_call(
        kern,
        out_shape=jax.ShapeDtypeStruct((t, d), F32),
        grid=(t // tm, d // tc),
        in_specs=[pl.BlockSpec((tm, tc), lambda i, j: (i, j)),
                  pl.BlockSpec((CONV_HALO, tc), lambda i, j: (jnp.maximum(i * halo_per_tile - 1, 0), j)),
                  pl.BlockSpec((CONV_HALO, tc), lambda i, j: (0, j)),
                  pl.BlockSpec((1, tc), lambda i, j: (0, j))],
        out_specs=pl.BlockSpec((tm, tc), lambda i, j: (i, j)),
        scratch_shapes=[pltpu.VMEM((tm + CONV_HALO, tc), F32)],
        compiler_params=_cparams(("parallel", "arbitrary")),
        name="dwconv",
    )(g, g, w_pad, b_dw.reshape(1, d))


def _layer_norm(z, g, b):
    mu = jnp.mean(z, axis=-1, keepdims=True)
    zc = z - mu
    var = jnp.mean(zc * zc, axis=-1, keepdims=True)
    return zc * lax.rsqrt(var + LN_EPS) * g + b


def _route_kernel(xres_ref, a_ref, w_ref, png_ref, pnb_ref, lng_ref, lnb_ref,
                  wr_hi_ref, wr_lo_ref, br_ref,
                  x1_ref, meta_ref, cnt_ref, carry_ref, *, pre_norm):
    tm = xres_ref.shape[0]

    @pl.when(pl.program_id(0) == 0)
    def _():
        carry_ref[...] = jnp.zeros_like(carry_ref)

    a = a_ref[...]
    if pre_norm:
        a = jax.nn.silu(_layer_norm(a.astype(F32), png_ref[...], pnb_ref[...]))
    y = jnp.dot(a.astype(BF16), w_ref[...], preferred_element_type=F32)
    x1 = _layer_norm(DEEPNORM_ALPHA * xres_ref[...] + y, lng_ref[...], lnb_ref[...])
    x1_ref[...] = x1

    x_hi = x1.astype(BF16)
    x_lo = (x1 - x_hi.astype(F32)).astype(BF16)
    logits = (jnp.dot(x_hi, wr_hi_ref[...], preferred_element_type=F32)
              + jnp.dot(x_lo, wr_hi_ref[...], preferred_element_type=F32)
              + jnp.dot(x_hi, wr_lo_ref[...], preferred_element_type=F32)
              + br_ref[...])
    lane = lax.broadcasted_iota(jnp.int32, logits.shape, 1)
    ninf = jnp.float32(-jnp.inf)
    gl = jnp.where(lane < N_GROUPS, logits, ninf)
    gmax = jnp.max(gl, axis=1, keepdims=True)
    g_sel = jnp.min(jnp.where(gl == gmax, lane, LANES), axis=1, keepdims=True)
    g_p = 1.0 / jnp.sum(jnp.exp(gl - gmax), axis=1, keepdims=True)
    lo = N_GROUPS + EXPERTS_PER_GROUP * g_sel
    el = jnp.where((lane >= lo) & (lane < lo + EXPERTS_PER_GROUP), logits, ninf)
    v1 = jnp.max(el, axis=1, keepdims=True)
    i1 = jnp.min(jnp.where(el == v1, lane, LANES), axis=1, keepdims=True)
    el2 = jnp.where(lane == i1, ninf, el)
    v2 = jnp.max(el2, axis=1, keepdims=True)
    i2 = jnp.min(jnp.where(el2 == v2, lane, LANES), axis=1, keepdims=True)
    t = jnp.exp(v2 - v1)
    w1 = g_p / (1.0 + t)
    w2 = g_p * t / (1.0 + t)
    e1 = i1 - N_GROUPS
    e2 = i2 - N_GROUPS

    onehot = ((lane == e1) | (lane == e2)).astype(BF16)
    r_i = lax.broadcasted_iota(jnp.int32, (tm, tm), 0)
    c_i = lax.broadcasted_iota(jnp.int32, (tm, tm), 1)
    lower = (c_i < r_i).astype(BF16)
    before = jnp.dot(lower, onehot, preferred_element_type=F32) + carry_ref[...]
    rank1 = jnp.sum(jnp.where(lane == e1, before, 0.0), axis=1, keepdims=True)
    rank2 = jnp.sum(jnp.where(lane == e2, before, 0.0), axis=1, keepdims=True)
    carry_ref[...] = carry_ref[...] + jnp.sum(onehot.astype(F32), axis=0, keepdims=True)
    cnt_ref[...] = carry_ref[...]

    meta = jnp.where(lane == 0, e1.astype(F32),
           jnp.where(lane == 1, e2.astype(F32),
           jnp.where(lane == 2, w1,
           jnp.where(lane == 3, w2,
           jnp.where(lane == 4, rank1,
           jnp.where(lane == 5, rank2, 0.0))))))
    meta_ref[...] = meta


def _proj_norm_route(xres, a, w_bf16, pn_g, pn_b, ln_g, ln_b, wr_hi, wr_lo, br, *, pre_norm):
    tm = TM_ROUTE
    t, d = xres.shape
    row = lambda i: (i, 0)
    fix = lambda i: (0, 0)
    vec = lambda v: v.reshape(1, d)
    kern = functools.partial(_route_kernel, pre_norm=pre_norm)
    return pl.pallas_call(
        kern,
        out_shape=(jax.ShapeDtypeStruct((t, d), F32),
                   jax.ShapeDtypeStruct((t, LANES), F32),
                   jax.ShapeDtypeStruct((1, LANES), F32)),
        grid=(t // tm,),
        in_specs=[pl.BlockSpec((tm, d), row),
                  pl.BlockSpec((tm, d), row),
                  pl.BlockSpec((d, d), fix),
                  pl.BlockSpec((1, d), fix), pl.BlockSpec((1, d), fix),
                  pl.BlockSpec((1, d), fix), pl.BlockSpec((1, d), fix),
                  pl.BlockSpec((d, LANES), fix), pl.BlockSpec((d, LANES), fix),
                  pl.BlockSpec((1, LANES), fix)],
        out_specs=(pl.BlockSpec((tm, d), row),
                   pl.BlockSpec((tm, LANES), row),
                   pl.BlockSpec((1, LANES), fix)),
        scratch_shapes=[pltpu.VMEM((1, LANES), F32)],
        compiler_params=_cparams(("arbitrary",)),
        name="proj_norm_route",
    )(xres, a, w_bf16, vec(pn_g), vec(pn_b), vec(ln_g), vec(ln_b), wr_hi, wr_lo, br)


def _row_copy(src_ref, src_row, dst_ref, dst_row, sem):
    return pltpu.make_async_copy(src_ref.at[pl.ds(src_row, 1), :], dst_ref.at[pl.ds(dst_row, 1), :], sem)


def _scatter_kernel(e_ref, r_ref, off_ref, x_ref, xs_in_ref, xs_ref, sem):
    del xs_in_ref
    tm = x_ref.shape[0]
    base = pl.program_id(0) * tm

    def dst(tok, k):
        idx = 2 * (base + tok) + k
        return off_ref[e_ref[idx]] + r_ref[idx]

    def start(tok, carry):
        _row_copy(x_ref, tok, xs_ref, dst(tok, 0), sem).start()
        _row_copy(x_ref, tok, xs_ref, dst(tok, 1), sem).start()
        return carry

    def wait(tok, carry):
        _row_copy(x_ref, tok, xs_ref, dst(tok, 0), sem).wait()
        _row_copy(x_ref, tok, xs_ref, dst(tok, 1), sem).wait()
        return carry

    lax.fori_loop(0, tm, start, 0)
    lax.fori_loop(0, tm, wait, 0)


def _scatter_rows(e12, r12, off, x1):
    tm = TM_MOVE
    t, d = x1.shape
    xs0 = jnp.zeros((N_SORTED_ROWS, d), F32)
    return pl.pallas_call(
        _scatter_kernel,
        out_shape=jax.ShapeDtypeStruct((N_SORTED_ROWS, d), F32),
        grid_spec=pltpu.PrefetchScalarGridSpec(
            num_scalar_prefetch=3,
            grid=(t // tm,),
            in_specs=[pl.BlockSpec((tm, d), lambda i, *_: (i, 0)),
                      pl.BlockSpec(memory_space=pl.ANY)],
            out_specs=pl.BlockSpec(memory_space=pl.ANY),
            scratch_shapes=[pltpu.SemaphoreType.DMA]),
        input_output_aliases={4: 0},
        compiler_params=_cparams(("arbitrary",)),
        name="moe_scatter",
    )(e12, r12, off, x1, xs0)


def _expert_kernel(te_ref, nu_ref, xs_ref, wg_ref, wu_ref, wd_ref, y_ref):
    del te_ref
    i = pl.program_id(0)

    @pl.when(i < nu_ref[0])
    def _():
        xb = xs_ref[...].astype(BF16)
        g = jnp.dot(xb, wg_ref[...], preferred_element_type=F32)
        u = jnp.dot(xb, wu_ref[...], preferred_element_type=F32)
        h = (jax.nn.silu(g) * u).astype(BF16)
        y_ref[...] = jnp.dot(h, wd_ref[...], preferred_element_type=F32)

    @pl.when(i >= nu_ref[0])
    def _():
        y_ref[...] = jnp.zeros_like(y_ref)


def _expert_mlp(tile_expert, n_used, xs, wg, wu, wd):
    tm = TM_EXPERT
    d, f = D_MODEL, D_EXPERT
    return pl.pallas_call(
        _expert_kernel,
        out_shape=jax.ShapeDtypeStruct((N_SORTED_ROWS, d), F32),
        grid_spec=pltpu.PrefetchScalarGridSpec(
            num_scalar_prefetch=2,
            grid=(N_EXPERT_TILES,),
            in_specs=[pl.BlockSpec((tm, d), lambda i, te, nu: (i, 0)),
                      pl.BlockSpec((None, d, f), lambda i, te, nu: (te[i], 0, 0)),
                      pl.BlockSpec((None, d, f), lambda i, te, nu: (te[i], 0, 0)),
                      pl.BlockSpec((None, f, d), lambda i, te, nu: (te[i], 0, 0))],
            out_specs=pl.BlockSpec((tm, d), lambda i, te, nu: (i, 0))),
        compiler_params=_cparams(("arbitrary",)),
        name="moe_experts",
    )(tile_expert, n_used, xs, wg, wu, wd)


def _combine_kernel(e_ref, r_ref, off_ref, x_ref, meta_ref, y_ref, lng_ref, lnb_ref, o_ref, ybuf_ref, sem):
    tm = x_ref.shape[0]
    base = pl.program_id(0) * tm

    def src(tok, k):
        idx = 2 * (base + tok) + k
        return off_ref[e_ref[idx]] + r_ref[idx]

    def start(tok, carry):
        _row_copy(y_ref, src(tok, 0), ybuf_ref.at[0], tok, sem).start()
        _row_copy(y_ref, src(tok, 1), ybuf_ref.at[1], tok, sem).start()
        return carry

    def wait(tok, carry):
        _row_copy(y_ref, src(tok, 0), ybuf_ref.at[0], tok, sem).wait()
        _row_copy(y_ref, src(tok, 1), ybuf_ref.at[1], tok, sem).wait()
        return carry

    lax.fori_loop(0, tm, start, 0)
    lax.fori_loop(0, tm, wait, 0)
    meta = meta_ref[...]
    lane = lax.broadcasted_iota(jnp.int32, meta.shape, 1)
    w1 = jnp.sum(jnp.where(lane == 2, meta, 0.0), axis=1, keepdims=True)
    w2 = jnp.sum(jnp.where(lane == 3, meta, 0.0), axis=1, keepdims=True)
    y = w1 * ybuf_ref[0] + w2 * ybuf_ref[1]
    o_ref[...] = _layer_norm(DEEPNORM_ALPHA * x_ref[...] + y, lng_ref[...], lnb_ref[...])


def _combine_norm(e12, r12, off, x1, meta, y, ln_g, ln_b):
    tm = TM_MOVE
    t, d = x1.shape
    return pl.pallas_call(
        _combine_kernel,
        out_shape=jax.ShapeDtypeStruct((t, d), F32),
        grid_spec=pltpu.PrefetchScalarGridSpec(
            num_scalar_prefetch=3,
            grid=(t // tm,),
            in_specs=[pl.BlockSpec((tm, d), lambda i, *_: (i, 0)),
                      pl.BlockSpec((tm, LANES), lambda i, *_: (i, 0)),
                      pl.BlockSpec(memory_space=pl.ANY),
                      pl.BlockSpec((1, d), lambda i, *_: (0, 0)),
                      pl.BlockSpec((1, d), lambda i, *_: (0, 0))],
            out_specs=pl.BlockSpec((tm, d), lambda i, *_: (i, 0)),
            scratch_shapes=[pltpu.VMEM((2, tm, d), F32), pltpu.SemaphoreType.DMA]),
        compiler_params=_cparams(("arbitrary",)),
        name="moe_combine_norm",
    )(e12, r12, off, x1, meta, y, ln_g.reshape(1, d), ln_b.reshape(1, d))


def _moe_layer(x1, meta, counts, wg, wu, wd, ln_g, ln_b):
    t = x1.shape[0]
    e12 = meta[:, 0:2].astype(jnp.int32).reshape(2 * t)
    r12 = meta[:, 4:6].astype(jnp.int32).reshape(2 * t)
    cnt = counts[0, :N_EXPERTS].astype(jnp.int32)
    padded = ((cnt + TM_EXPERT - 1) // TM_EXPERT) * TM_EXPERT
    ends = jnp.cumsum(padded)
    off = (ends - padded).astype(jnp.int32)
    tile_start = jnp.arange(N_EXPERT_TILES, dtype=jnp.int32) * TM_EXPERT
    tile_expert = jnp.minimum(jnp.searchsorted(ends, tile_start, side="right"), N_EXPERTS - 1).astype(jnp.int32)
    n_used = (ends[-1:] // TM_EXPERT).astype(jnp.int32)
    xs = _scatter_rows(e12, r12, off, x1)
    y = _expert_mlp(tile_expert, n_used, xs, wg, wu, wd)
    return _combine_norm(e12, r12, off, x1, meta, y, ln_g, ln_b)


def _router_weights(w_grp, b_grp, w_rt, b_rt):
    d = w_grp.shape[0]
    pad = LANES - N_GROUPS - N_EXPERTS
    wr = jnp.concatenate([w_grp, w_rt.reshape(d, N_EXPERTS), jnp.zeros((d, pad), F32)], axis=1)
    br = jnp.concatenate([b_grp, b_rt.reshape(N_EXPERTS), jnp.zeros((pad,), F32)]).reshape(1, LANES)
    wr_hi = _round_to_bf16(wr)
    wr_lo = _round_to_bf16(wr - wr_hi)
    return wr_hi.astype(BF16), wr_lo.astype(BF16), br


def kernel(x, attn_w_qkv, attn_w_o, conv_w_pw1, conv_b_pw1, conv_w_dw, conv_b_dw, conv_ln_g, conv_ln_b,
           conv_w_pw2, mix_ln_g, mix_ln_b, moe_w_grp, moe_b_grp, moe_w_rt, moe_b_rt, moe_w_gate, moe_w_up,
           moe_w_down, ffn_ln_g, ffn_ln_b):
    b, s, d = x.shape
    assert (b, s, d) == (1, SEQ, D_MODEL)
    xt = x.reshape(s, d)
    ones = jnp.ones((d,), F32)
    zeros = jnp.zeros((d,), F32)

    def moe_weights(i):
        shp_in = (N_EXPERTS, D_MODEL, D_EXPERT)
        return (moe_w_gate[i].reshape(shp_in).astype(BF16), moe_w_up[i].reshape(shp_in).astype(BF16),
                moe_w_down[i].reshape(N_EXPERTS, D_EXPERT, D_MODEL).astype(BF16))

    qkv = _qkv_proj(xt, attn_w_qkv[0].astype(BF16))
    o = _moba_attention(qkv, _moba_key_table())
    wr_hi, wr_lo, br = _router_weights(moe_w_grp[0], moe_b_grp[0], moe_w_rt[0], moe_b_rt[0])
    x1, meta, counts = _proj_norm_route(xt, o, attn_w_o[0].astype(BF16), ones, zeros, mix_ln_g[0], mix_ln_b[0],
                                        wr_hi, wr_lo, br, pre_norm=False)
    x2 = _moe_layer(x1, meta, counts, *moe_weights(0), ffn_ln_g[0], ffn_ln_b[0])

    g = _pw1_glu(x2, conv_w_pw1[0].astype(BF16), conv_b_pw1[0])
    c = _dwconv(g, conv_w_dw[0], conv_b_dw[0])
    wr_hi, wr_lo, br = _router_weights(moe_w_grp[1], moe_b_grp[1], moe_w_rt[1], moe_b_rt[1])
    x3, meta, counts = _proj_norm_route(x2, c, conv_w_pw2[0].astype(BF16), conv_ln_g[0], conv_ln_b[0],
                                        mix_ln_g[1], mix_ln_b[1], wr_hi, wr_lo, br, pre_norm=True)
    x4 = _moe_layer(x3, meta, counts, *moe_weights(1), ffn_ln_g[1], ffn_ln_b[1])
    return x4.reshape(b, s, d)
```

```python
import functools

import jax
import jax.numpy as jnp
from jax import lax
from jax.experimental import pallas as pl
from jax.experimental.pallas import tpu as pltpu

F32 = jnp.float32
BF16 = jnp.bfloat16

D_MODEL = 2048
SEQ = 16384
DEPTH = 2
N_HEADS = 16
HEAD_DIM = D_MODEL // N_HEADS
MOBA_BLOCK = 256
MOBA_TOPK = 3
N_BLOCKS = SEQ // MOBA_BLOCK
CONV_WIDTH = 31
N_GROUPS = 4
EXPERTS_PER_GROUP = 8
N_EXPERTS = N_GROUPS * EXPERTS_PER_GROUP
D_EXPERT = D_MODEL // 4
LN_EPS = 1e-5
NEG_INF = -1e30
DEEPNORM_ALPHA = (2.0 * DEPTH) ** 0.25

LANES = 128
CONV_HALO = 32
TM_ROUTE = 256
TM_MOVE = 256
TM_EXPERT = 256
N_EXPERT_TILES = (2 * SEQ) // TM_EXPERT + N_EXPERTS
N_SORTED_ROWS = N_EXPERT_TILES * TM_EXPERT
VMEM_LIMIT = 56 * 1024 * 1024


def _round_to_bf16(v):
    return lax.reduce_precision(v, exponent_bits=8, mantissa_bits=7)


def _cparams(sem):
    return pltpu.CompilerParams(dimension_semantics=sem, vmem_limit_bytes=VMEM_LIMIT)


def _qkv_kernel(x_ref, w_ref, o_ref, *, n_scaled_tiles, scale):
    acc = jnp.dot(x_ref[...].astype(BF16), w_ref[...], preferred_element_type=F32)
    s = jnp.where(pl.program_id(1) < n_scaled_tiles, scale, 1.0).astype(F32)
    o_ref[...] = (acc * s).astype(o_ref.dtype)


def _qkv_proj(x, w_bf16):
    tm, tn = 1024, 1024
    t, d = x.shape
    n = w_bf16.shape[1]
    kern = functools.partial(_qkv_kernel, n_scaled_tiles=D_MODEL // tn, scale=HEAD_DIM ** -0.5)
    return pl.pallas_call(
        kern,
        out_shape=jax.ShapeDtypeStruct((t, n), BF16),
        grid=(t // tm, n // tn),
        in_specs=[pl.BlockSpec((tm, d), lambda i, j: (i, 0)),
                  pl.BlockSpec((d, tn), lambda i, j: (0, j))],
        out_specs=pl.BlockSpec((tm, tn), lambda i, j: (i, j)),
        compiler_params=_cparams(("parallel", "arbitrary")),
        name="qkv_proj",
    )(x, w_bf16)


def _glu_kernel(x_ref, wa_ref, wb_ref, ba_ref, bb_ref, o_ref):
    xb = x_ref[...].astype(BF16)
    a = jnp.dot(xb, wa_ref[...], preferred_element_type=F32) + ba_ref[...]
    b = jnp.dot(xb, wb_ref[...], preferred_element_type=F32) + bb_ref[...]
    o_ref[...] = a * jax.nn.sigmoid(b)


def _pw1_glu(x, w_bf16, bias):
    tm, tn = 1024, 512
    t, d = x.shape
    nb = d // tn
    bias2 = bias.reshape(1, 2 * d)
    return pl.pallas_call(
        _glu_kernel,
        out_shape=jax.ShapeDtypeStruct((t, d), F32),
        grid=(t // tm, nb),
        in_specs=[pl.BlockSpec((tm, d), lambda i, j: (i, 0)),
                  pl.BlockSpec((d, tn), lambda i, j: (0, j)),
                  pl.BlockSpec((d, tn), lambda i, j: (0, j + nb)),
                  pl.BlockSpec((1, tn), lambda i, j: (0, j)),
                  pl.BlockSpec((1, tn), lambda i, j: (0, j + nb))],
        out_specs=pl.BlockSpec((tm, tn), lambda i, j: (i, j)),
        compiler_params=_cparams(("parallel", "arbitrary")),
        name="pw1_glu",
    )(x, w_bf16, w_bf16, bias2, bias2)


N_SLOPE_SPLIT = 3
KPOS_LO = 128
HEADS_PER_STEP = 2
CHUNK_BLOCKS = 4
BIAS_LANE0 = N_BLOCKS
N_BIAS_LANES = 2 * N_SLOPE_SPLIT


def _moba_key_table():
    kpos = jnp.arange(SEQ, dtype=jnp.int32)
    onehot = (kpos[:, None] // MOBA_BLOCK == jnp.arange(N_BLOCKS)[None, :]).astype(BF16)
    hi = ((kpos // KPOS_LO) * KPOS_LO).astype(BF16)[:, None]
    lo = (kpos % KPOS_LO).astype(BF16)[:, None]
    pad = jnp.zeros((SEQ, LANES - N_BLOCKS - N_BIAS_LANES), BF16)
    return jnp.concatenate([onehot] + [hi] * N_SLOPE_SPLIT + [lo] * N_SLOPE_SPLIT + [pad], axis=1)


def _moba_slope_table():
    slopes = 2.0 ** (-8.0 * jnp.arange(1, N_HEADS + 1, dtype=F32) / N_HEADS)
    pieces = []
    rem = slopes
    for _ in range(N_SLOPE_SPLIT):
        p = _round_to_bf16(rem)
        pieces.append(p[:, None])
        rem = rem - p
    tab = jnp.concatenate([jnp.zeros((N_HEADS, BIAS_LANE0), F32)] + pieces + pieces
                          + [jnp.zeros((N_HEADS, LANES - BIAS_LANE0 - N_BIAS_LANES), F32)], axis=1)
    return tab.reshape(N_HEADS // HEADS_PER_STEP, HEADS_PER_STEP, LANES)


def _moba_kernel(q_ref, k_ref, v_ref, c_ref, sl_ref, o_ref, kmean_ref, qa_ref, m_ref, l_ref, acc_ref):
    blk = MOBA_BLOCK
    hp_n = HEADS_PER_STEP
    own = pl.program_id(1)
    nt = (((1,), (1,)), ((), ()))
    hcols = lambda hp: slice(hp * HEAD_DIM, (hp + 1) * HEAD_DIM)

    @pl.when(own == 0)
    def _():
        kmean_ref[...] = jnp.zeros_like(kmean_ref)

        def body(j, carry):
            kb = k_ref[pl.ds(pl.multiple_of(j * blk, blk), blk), :].astype(F32)
            mean = jnp.mean(kb, axis=0, keepdims=True)
            for hp in range(hp_n):
                kmean_ref[hp, pl.ds(j, 1), :] = mean[:, hcols(hp)]
            return carry

        lax.fori_loop(0, N_BLOCKS, body, 0)

    lane = lax.broadcasted_iota(jnp.int32, (blk, LANES), 1)
    minus_inf = jnp.float32(-jnp.inf)
    own_rows = pl.ds(pl.multiple_of(own * blk, blk), blk)
    r_i = lax.broadcasted_iota(jnp.int32, (blk, blk), 0)
    c_i = lax.broadcasted_iota(jnp.int32, (blk, blk), 1)
    for hp in range(hp_n):
        q = q_ref[:, hcols(hp)]
        km = kmean_ref[hp]
        km_hi = km.astype(BF16)
        km_lo = (km - km_hi.astype(F32)).astype(BF16)
        gate = (lax.dot_general(q, km_hi, nt, preferred_element_type=F32)
                + lax.dot_general(q, km_lo, nt, preferred_element_type=F32))
        gm = jnp.where(lane < own, gate, minus_inf)
        sel = lane < 0
        for r in range(MOBA_TOPK):
            mx = jnp.max(gm, axis=1, keepdims=True)
            idx = jnp.min(jnp.where(gm == mx, lane, LANES), axis=1, keepdims=True)
            hit = lane == idx
            sel = sel | (hit & (r < own))
            gm = jnp.where(hit, minus_inf, gm)
        slope_lanes = jnp.broadcast_to(sl_ref[pl.ds(hp, 1), :], (blk, LANES)).astype(BF16)
        aug = jnp.where(lane < N_BLOCKS, jnp.where(sel, 0.0, NEG_INF).astype(BF16), slope_lanes)
        qa_ref[hp] = jnp.concatenate([q, aug], axis=1)

        aug_own = jnp.where(lane < N_BLOCKS, jnp.zeros_like(slope_lanes), slope_lanes)
        k_aug = jnp.concatenate([k_ref[own_rows, hcols(hp)], c_ref[own_rows, :]], axis=1)
        s = lax.dot_general(jnp.concatenate([q, aug_own], axis=1), k_aug, nt, preferred_element_type=F32)
        s = jnp.where(r_i >= c_i, s, NEG_INF)
        m0 = jnp.max(s, axis=1, keepdims=True)
        p = jnp.exp(s - m0)
        m_ref[hp] = jnp.broadcast_to(m0, (blk, LANES))
        l_ref[hp] = jnp.broadcast_to(jnp.sum(p, axis=1, keepdims=True), (blk, LANES))
        acc_ref[hp] = jnp.dot(p.astype(BF16), v_ref[own_rows, hcols(hp)], preferred_element_type=F32)

    width = CHUNK_BLOCKS * blk

    def body(c, carry):
        rows = pl.ds(pl.multiple_of(c * width, width), width)
        c_rows = c_ref[rows, :]
        for hp in range(hp_n):
            k_aug = jnp.concatenate([k_ref[rows, hcols(hp)], c_rows], axis=1)
            s = lax.dot_general(qa_ref[hp], k_aug, nt, preferred_element_type=F32)
            m_old = m_ref[hp]
            m_new = jnp.maximum(m_old, jnp.max(s, axis=1, keepdims=True))
            alpha = jnp.exp(m_old - m_new)
            p = jnp.exp(s - jnp.tile(m_new, (1, width // LANES)))
            l_ref[hp] = alpha * l_ref[hp] + jnp.sum(p, axis=1, keepdims=True)
            acc_ref[hp] = alpha * acc_ref[hp] + jnp.dot(p.astype(BF16), v_ref[rows, hcols(hp)],
                                                        preferred_element_type=F32)
            m_ref[hp] = m_new
        return carry

    n_chunks = lax.div(own + (CHUNK_BLOCKS - 1), jnp.int32(CHUNK_BLOCKS))
    lax.fori_loop(0, n_chunks, body, 0)
    for hp in range(hp_n):
        o_ref[:, hcols(hp)] = (acc_ref[hp] / l_ref[hp]).astype(o_ref.dtype)


def _moba_attention(qkv, key_table, slope_table):
    blk = MOBA_BLOCK
    hp_n = HEADS_PER_STEP
    w = hp_n * HEAD_DIM
    n_groups = N_HEADS // hp_n
    return pl.pallas_call(
        _moba_kernel,
        out_shape=jax.ShapeDtypeStruct((SEQ, D_MODEL), BF16),
        grid=(n_groups, N_BLOCKS),
        in_specs=[pl.BlockSpec((blk, w), lambda h, i: (i, h)),
                  pl.BlockSpec((SEQ, w), lambda h, i: (0, n_groups + h)),
                  pl.BlockSpec((SEQ, w), lambda h, i: (0, 2 * n_groups + h)),
                  pl.BlockSpec((SEQ, LANES), lambda h, i: (0, 0)),
                  pl.BlockSpec((None, hp_n, LANES), lambda h, i: (h, 0, 0))],
        out_specs=pl.BlockSpec((blk, w), lambda h, i: (i, h)),
        scratch_shapes=[pltpu.VMEM((hp_n, LANES, HEAD_DIM), F32),
                        pltpu.VMEM((hp_n, blk, 2 * LANES), BF16),
                        pltpu.VMEM((hp_n, blk, LANES), F32),
                        pltpu.VMEM((hp_n, blk, LANES), F32),
                        pltpu.VMEM((hp_n, blk, HEAD_DIM), F32)],
        compiler_params=_cparams(("arbitrary", "arbitrary")),
        name="moba_attention",
    )(qkv, qkv, qkv, key_table, slope_table)


def _dwconv_kernel(cur_ref, prev_ref, w_ref, b_ref, o_ref, win_ref, *, row_chunk):
    tm = cur_ref.shape[0]
    keep = (pl.program_id(0) > 0).astype(F32)
    win_ref[pl.ds(0, CONV_HALO), :] = prev_ref[...] * keep
    win_ref[pl.ds(CONV_HALO, tm), :] = cur_ref[...]
    first = CONV_HALO - (CONV_WIDTH - 1)
    for rc in range(tm // row_chunk):
        acc = jnp.broadcast_to(b_ref[...], (row_chunk, cur_ref.shape[1]))
        for k in range(CONV_WIDTH):
            acc = acc + w_ref[pl.ds(k, 1), :] * win_ref[pl.ds(first + k + rc * row_chunk, row_chunk), :]
        o_ref[pl.ds(rc * row_chunk, row_chunk), :] = acc


def _dwconv(g, w_dw, b_dw):
    tm, tc = 512, 512
    t, d = g.shape
    w_pad = jnp.concatenate([w_dw, jnp.zeros((CONV_HALO - CONV_WIDTH, d), F32)], axis=0)
    halo_per_tile = tm // CONV_HALO
    kern = functools.partial(_dwconv_kernel, row_chunk=64)
    return pl.pallas_call(
        kern,
        out_shape=jax.ShapeDtypeStruct((t, d), F32),
        grid=(t // tm, d // tc),
        in_specs=[pl.BlockSpec((tm, tc), lambda i, j: (i, j)),
                  pl.BlockSpec((CONV_HALO, tc), lambda i, j: (jnp.maximum(i * halo_per_tile - 1, 0), j)),
                  pl.BlockSpec((CONV_HALO, tc), lambda i, j: (0, j)),
                  pl.BlockSpec((1, tc), lambda i, j: (0, j))],
        out_specs=pl.BlockSpec((tm, tc), lambda i, j: (i, j)),
        scratch_shapes=[pltpu.VMEM((tm + CONV_HALO, tc), F32)],
        compiler_params=_cparams(("parallel", "arbitrary")),
        name="dwconv",
    )(g, g, w_pad, b_dw.reshape(1, d))


def _layer_norm(z, g, b):
    mu = jnp.mean(z, axis=-1, keepdims=True)
    zc = z - mu
    var = jnp.mean(zc * zc, axis=-1, keepdims=True)
    return zc * lax.rsqrt(var + LN_EPS) * g + b


def _route_kernel(xres_ref, a_ref, w_ref, png_ref, pnb_ref, lng_ref, lnb_ref,
                  wr_hi_ref, wr_lo_ref, br_ref,
                  x1_ref, meta_ref, cnt_ref, carry_ref, *, pre_norm):
    tm = xres_ref.shape[0]

    @pl.when(pl.program_id(0) == 0)
    def _():
        carry_ref[...] = jnp.zeros_like(carry_ref)

    a = a_ref[...]
    if pre_norm:
        a = jax.nn.silu(_layer_norm(a.astype(F32), png_ref[...], pnb_ref[...]))
    y = jnp.dot(a.astype(BF16), w_ref[...], preferred_element_type=F32)
    x1 = _layer_norm(DEEPNORM_ALPHA * xres_ref[...] + y, lng_ref[...], lnb_ref[...])
    x1_ref[...] = x1

    x_hi = x1.astype(BF16)
    x_lo = (x1 - x_hi.astype(F32)).astype(BF16)
    logits = (jnp.dot(x_hi, wr_hi_ref[...], preferred_element_type=F32)
              + jnp.dot(x_lo, wr_hi_ref[...], preferred_element_type=F32)
              + jnp.dot(x_hi, wr_lo_ref[...], preferred_element_type=F32)
              + br_ref[...])
    lane = lax.broadcasted_iota(jnp.int32, logits.shape, 1)
    ninf = jnp.float32(-jnp.inf)
    gl = jnp.where(lane < N_GROUPS, logits, ninf)
    gmax = jnp.max(gl, axis=1, keepdims=True)
    g_sel = jnp.min(jnp.where(gl == gmax, lane, LANES), axis=1, keepdims=True)
    g_p = 1.0 / jnp.sum(jnp.exp(gl - gmax), axis=1, keepdims=True)
    lo = N_GROUPS + EXPERTS_PER_GROUP * g_sel
    el = jnp.where((lane >= lo) & (lane < lo + EXPERTS_PER_GROUP), logits, ninf)
    v1 = jnp.max(el, axis=1, keepdims=True)
    i1 = jnp.min(jnp.where(el == v1, lane, LANES), axis=1, keepdims=True)
    el2 = jnp.where(lane == i1, ninf, el)
    v2 = jnp.max(el2, axis=1, keepdims=True)
    i2 = jnp.min(jnp.where(el2 == v2, lane, LANES), axis=1, keepdims=True)
    t = jnp.exp(v2 - v1)
    w1 = g_p / (1.0 + t)
    w2 = g_p * t / (1.0 + t)
    e1 = i1 - N_GROUPS
    e2 = i2 - N_GROUPS

    onehot = ((lane == e1) | (lane == e2)).astype(BF16)
    r_i = lax.broadcasted_iota(jnp.int32, (tm, tm), 0)
    c_i = lax.broadcasted_iota(jnp.int32, (tm, tm), 1)
    lower = (c_i < r_i).astype(BF16)
    before = jnp.dot(lower, onehot, preferred_element_type=F32) + carry_ref[...]
    rank1 = jnp.sum(jnp.where(lane == e1, before, 0.0), axis=1, keepdims=True)
    rank2 = jnp.sum(jnp.where(lane == e2, before, 0.0), axis=1, keepdims=True)
    carry_ref[...] = carry_ref[...] + jnp.sum(onehot.astype(F32), axis=0, keepdims=True)
    cnt_ref[...] = carry_ref[...]

    meta = jnp.where(lane == 0, e1.astype(F32),
           jnp.where(lane == 1, e2.astype(F32),
           jnp.where(lane == 2, w1,
           jnp.where(lane == 3, w2,
           jnp.where(lane == 4, rank1,
           jnp.where(lane == 5, rank2, 0.0))))))
    meta_ref[...] = meta


def _proj_norm_route(xres, a, w_bf16, pn_g, pn_b, ln_g, ln_b, wr_hi, wr_lo, br, *, pre_norm):
    tm = TM_ROUTE
    t, d = xres.shape
    row = lambda i: (i, 0)
    fix = lambda i: (0, 0)
    vec = lambda v: v.reshape(1, d)
    kern = functools.partial(_route_kernel, pre_norm=pre_norm)
    return pl.pallas_call(
        kern,
        out_shape=(jax.ShapeDtypeStruct((t, d), F32),
                   jax.ShapeDtypeStruct((t, LANES), F32),
                   jax.ShapeDtypeStruct((1, LANES), F32)),
        grid=(t // tm,),
        in_specs=[pl.BlockSpec((tm, d), row),
                  pl.BlockSpec((tm, d), row),
                  pl.BlockSpec((d, d), fix),
                  pl.BlockSpec((1, d), fix), pl.BlockSpec((1, d), fix),
                  pl.BlockSpec((1, d), fix), pl.BlockSpec((1, d), fix),
                  pl.BlockSpec((d, LANES), fix), pl.BlockSpec((d, LANES), fix),
                  pl.BlockSpec((1, LANES), fix)],
        out_specs=(pl.BlockSpec((tm, d), row),
                   pl.BlockSpec((tm, LANES), row),
                   pl.BlockSpec((1, LANES), fix)),
        scratch_shapes=[pltpu.VMEM((1, LANES), F32)],
        compiler_params=_cparams(("arbitrary",)),
        name="proj_norm_route",
    )(xres, a, w_bf16, vec(pn_g), vec(pn_b), vec(ln_g), vec(ln_b), wr_hi, wr_lo, br)


def _row_copy(src_ref, src_row, dst_ref, dst_row, sem):
    return pltpu.make_async_copy(src_ref.at[pl.ds(src_row, 1), :], dst_ref.at[pl.ds(dst_row, 1), :], sem)


def _scatter_kernel(e_ref, r_ref, off_ref, x_ref, xs_in_ref, xs_ref, sem):
    del xs_in_ref
    tm = x_ref.shape[0]
    base = pl.program_id(0) * tm

    def dst(tok, k):
        idx = 2 * (base + tok) + k
        return off_ref[e_ref[idx]] + r_ref[idx]

    def start(tok, carry):
        _row_copy(x_ref, tok, xs_ref, dst(tok, 0), sem).start()
        _row_copy(x_ref, tok, xs_ref, dst(tok, 1), sem).start()
        return carry

    def wait(tok, carry):
        _row_copy(x_ref, tok, xs_ref, dst(tok, 0), sem).wait()
        _row_copy(x_ref, tok, xs_ref, dst(tok, 1), sem).wait()
        return carry

    lax.fori_loop(0, tm, start, 0)
    lax.fori_loop(0, tm, wait, 0)


def _scatter_rows(e12, r12, off, x1):
    tm = TM_MOVE
    t, d = x1.shape
    xs0 = jnp.zeros((N_SORTED_ROWS, d), F32)
    return pl.pallas_call(
        _scatter_kernel,
        out_shape=jax.ShapeDtypeStruct((N_SORTED_ROWS, d), F32),
        grid_spec=pltpu.PrefetchScalarGridSpec(
            num_scalar_prefetch=3,
            grid=(t // tm,),
            in_specs=[pl.BlockSpec((tm, d), lambda i, *_: (i, 0)),
                      pl.BlockSpec(memory_space=pl.ANY)],
            out_specs=pl.BlockSpec(memory_space=pl.ANY),
            scratch_shapes=[pltpu.SemaphoreType.DMA]),
        input_output_aliases={4: 0},
        compiler_params=_cparams(("arbitrary",)),
        name="moe_scatter",
    )(e12, r12, off, x1, xs0)


def _expert_kernel(te_ref, nu_ref, xs_ref, wg_ref, wu_ref, wd_ref, y_ref):
    del te_ref
    i = pl.program_id(0)

    @pl.when(i < nu_ref[0])
    def _():
        xb = xs_ref[...].astype(BF16)
        g = jnp.dot(xb, wg_ref[...], preferred_element_type=F32)
        u = jnp.dot(xb, wu_ref[...], preferred_element_type=F32)
        h = (jax.nn.silu(g) * u).astype(BF16)
        y_ref[...] = jnp.dot(h, wd_ref[...], preferred_element_type=F32)

    @pl.when(i >= nu_ref[0])
    def _():
        y_ref[...] = jnp.zeros_like(y_ref)


def _expert_mlp(tile_expert, n_used, xs, wg, wu, wd):
    tm = TM_EXPERT
    d, f = D_MODEL, D_EXPERT
    return pl.pallas_call(
        _expert_kernel,
        out_shape=jax.ShapeDtypeStruct((N_SORTED_ROWS, d), F32),
        grid_spec=pltpu.PrefetchScalarGridSpec(
            num_scalar_prefetch=2,
            grid=(N_EXPERT_TILES,),
            in_specs=[pl.BlockSpec((tm, d), lambda i, te, nu: (i, 0)),
                      pl.BlockSpec((None, d, f), lambda i, te, nu: (te[i], 0, 0)),
                      pl.BlockSpec((None, d, f), lambda i, te, nu: (te[i], 0, 0)),
                      pl.BlockSpec((None, f, d), lambda i, te, nu: (te[i], 0, 0))],
            out_specs=pl.BlockSpec((tm, d), lambda i, te, nu: (i, 0))),
        compiler_params=_cparams(("arbitrary",)),
        name="moe_experts",
    )(tile_expert, n_used, xs, wg, wu, wd)


def _combine_kernel(e_ref, r_ref, off_ref, x_ref, meta_ref, y_ref, lng_ref, lnb_ref, o_ref, ybuf_ref, sem):
    tm = x_ref.shape[0]
    base = pl.program_id(0) * tm

    def src(tok, k):
        idx = 2 * (base + tok) + k
        return off_ref[e_ref[idx]] + r_ref[idx]

    def start(tok, carry):
        _row_copy(y_ref, src(tok, 0), ybuf_ref.at[0], tok, sem).start()
        _row_copy(y_ref, src(tok, 1), ybuf_ref.at[1], tok, sem).start()
        return carry

    def wait(tok, carry):
        _row_copy(y_ref, src(tok, 0), ybuf_ref.at[0], tok, sem).wait()
        _row_copy(y_ref, src(tok, 1), ybuf_ref.at[1], tok, sem).wait()
        return carry

    lax.fori_loop(0, tm, start, 0)
    lax.fori_loop(0, tm, wait, 0)
    meta = meta_ref[...]
    lane = lax.broadcasted_iota(jnp.int32, meta.shape, 1)
    w1 = jnp.sum(jnp.where(lane == 2, meta, 0.0), axis=1, keepdims=True)
    w2 = jnp.sum(jnp.where(lane == 3, meta, 0.0), axis=1, keepdims=True)
    y = w1 * ybuf_ref[0] + w2 * ybuf_ref[1]
    o_ref[...] = _layer_norm(DEEPNORM_ALPHA * x_ref[...] + y, lng_ref[...], lnb_ref[...])


def _combine_norm(e12, r12, off, x1, meta, y, ln_g, ln_b):
    tm = TM_MOVE
    t, d = x1.shape
    return pl.pallas_call(
        _combine_kernel,
        out_shape=jax.ShapeDtypeStruct((t, d), F32),
        grid_spec=pltpu.PrefetchScalarGridSpec(
            num_scalar_prefetch=3,
            grid=(t // tm,),
            in_specs=[pl.BlockSpec((tm, d), lambda i, *_: (i, 0)),
                      pl.BlockSpec((tm, LANES), lambda i, *_: (i, 0)),
                      pl.BlockSpec(memory_space=pl.ANY),
                      pl.BlockSpec((1, d), lambda i, *_: (0, 0)),
                      pl.BlockSpec((1, d), lambda i, *_: (0, 0))],
            out_specs=pl.BlockSpec((tm, d), lambda i, *_: (i, 0)),
            scratch_shapes=[pltpu.VMEM((2, tm, d), F32), pltpu.SemaphoreType.DMA]),
        compiler_params=_cparams(("arbitrary",)),
        name="moe_combine_norm",
    )(e12, r12, off, x1, meta, y, ln_g.reshape(1, d), ln_b.reshape(1, d))


def _moe_layer(x1, meta, counts, wg, wu, wd, ln_g, ln_b):
    t = x1.shape[0]
    e12 = meta[:, 0:2].astype(jnp.int32).reshape(2 * t)
    r12 = meta[:, 4:6].astype(jnp.int32).reshape(2 * t)
    cnt = counts[0, :N_EXPERTS].astype(jnp.int32)
    padded = ((cnt + TM_EXPERT - 1) // TM_EXPERT) * TM_EXPERT
    ends = jnp.cumsum(padded)
    off = (ends - padded).astype(jnp.int32)
    tile_start = jnp.arange(N_EXPERT_TILES, dtype=jnp.int32) * TM_EXPERT
    tile_expert = jnp.minimum(jnp.searchsorted(ends, tile_start, side="right"), N_EXPERTS - 1).astype(jnp.int32)
    n_used = (ends[-1:] // TM_EXPERT).astype(jnp.int32)
    xs = _scatter_rows(e12, r12, off, x1)
    y = _expert_mlp(tile_expert, n_used, xs, wg, wu, wd)
    return _combine_norm(e12, r12, off, x1, meta, y, ln_g, ln_b)


def _router_weights(w_grp, b_grp, w_rt, b_rt):
    d = w_grp.shape[0]
    pad = LANES - N_GROUPS - N_EXPERTS
    wr = jnp.concatenate([w_grp, w_rt.reshape(d, N_EXPERTS), jnp.zeros((d, pad), F32)], axis=1)
    br = jnp.concatenate([b_grp, b_rt.reshape(N_EXPERTS), jnp.zeros((pad,), F32)]).reshape(1, LANES)
    wr_hi = _round_to_bf16(wr)
    wr_lo = _round_to_bf16(wr - wr_hi)
    return wr_hi.astype(BF16), wr_lo.astype(BF16), br


def kernel(x, attn_w_qkv, attn_w_o, conv_w_pw1, conv_b_pw1, conv_w_dw, conv_b_dw, conv_ln_g, conv_ln_b,
           conv_w_pw2, mix_ln_g, mix_ln_b, moe_w_grp, moe_b_grp, moe_w_rt, moe_b_rt, moe_w_gate, moe_w_up,
           moe_w_down, ffn_ln_g, ffn_ln_b):
    b, s, d = x.shape
    assert (b, s, d) == (1, SEQ, D_MODEL)
    xt = x.reshape(s, d)
    ones = jnp.ones((d,), F32)
    zeros = jnp.zeros((d,), F32)

    def moe_weights(i):
        shp_in = (N_EXPERTS, D_MODEL, D_EXPERT)
        return (moe_w_gate[i].reshape(shp_in).astype(BF16), moe_w_up[i].reshape(shp_in).astype(BF16),
                moe_w_down[i].reshape(N_EXPERTS, D_EXPERT, D_MODEL).astype(BF16))

    qkv = _qkv_proj(xt, attn_w_qkv[0].astype(BF16))
    o = _moba_attention(qkv, _moba_key_table(), _moba_slope_table())
    wr_hi, wr_lo, br = _router_weights(moe_w_grp[0], moe_b_grp[0], moe_w_rt[0], moe_b_rt[0])
    x1, meta, counts = _proj_norm_route(xt, o, attn_w_o[0].astype(BF16), ones, zeros, mix_ln_g[0], mix_ln_b[0],
                                        wr_hi, wr_lo, br, pre_norm=False)
    x2 = _moe_layer(x1, meta, counts, *moe_weights(0), ffn_ln_g[0], ffn_ln_b[0])

    g = _pw1_glu(x2, conv_w_pw1[0].astype(BF16), conv_b_pw1[0])
    c = _dwconv(g, conv_w_dw[0], conv_b_dw[0])
    wr_hi, wr_lo, br = _router_weights(moe_w_grp[1], moe_b_grp[1], moe_w_rt[1], moe_b_rt[1])
    x3, meta, counts = _proj_norm_route(x2, c, conv_w_pw2[0].astype(BF16), conv_ln_g[0], conv_ln_b[0],
                                        mix_ln_g[1], mix_ln_b[1], wr_hi, wr_lo, br, pre_norm=True)
    x4 = _moe_layer(x3, meta, counts, *moe_weights(1), ffn_ln_g[1], ffn_ln_b[1])
    return x4.reshape(b, s, d)
```

```python
import functools

import jax
import jax.numpy as jnp
from jax import lax
from jax.experimental import pallas as pl
from jax.experimental.pallas import tpu as pltpu

F32 = jnp.float32
BF16 = jnp.bfloat16

D_MODEL = 2048
SEQ = 16384
DEPTH = 2
N_HEADS = 16
HEAD_DIM = D_MODEL // N_HEADS
MOBA_BLOCK = 256
MOBA_TOPK = 3
N_BLOCKS = SEQ // MOBA_BLOCK
CONV_WIDTH = 31
N_GROUPS = 4
EXPERTS_PER_GROUP = 8
N_EXPERTS = N_GROUPS * EXPERTS_PER_GROUP
D_EXPERT = D_MODEL // 4
LN_EPS = 1e-5
NEG_INF = -1e30
DEEPNORM_ALPHA = (2.0 * DEPTH) ** 0.25

LANES = 128
SUBLANES = 8
CONV_HALO = 32
TM_ROUTE = 256
TM_MOVE = 256
DMA_LOOP_UNROLL = 8
TM_EXPERT = 256
N_EXPERT_TILES = (2 * SEQ) // TM_EXPERT + N_EXPERTS
N_SORTED_ROWS = N_EXPERT_TILES * TM_EXPERT
VMEM_LIMIT = 56 * 1024 * 1024


def _round_to_bf16(v):
    return lax.reduce_precision(v, exponent_bits=8, mantissa_bits=7)


def _cparams(sem):
    return pltpu.CompilerParams(dimension_semantics=sem, vmem_limit_bytes=VMEM_LIMIT)


def _qkv_kernel(x_ref, w_ref, o_ref, *, n_scaled_tiles, scale):
    acc = jnp.dot(x_ref[...].astype(BF16), w_ref[...], preferred_element_type=F32)
    s = jnp.where(pl.program_id(1) < n_scaled_tiles, scale, 1.0).astype(F32)
    o_ref[...] = (acc * s).astype(o_ref.dtype)


def _qkv_proj(x, w_bf16):
    tm, tn = 1024, 1024
    t, d = x.shape
    n = w_bf16.shape[1]
    kern = functools.partial(_qkv_kernel, n_scaled_tiles=D_MODEL // tn, scale=HEAD_DIM ** -0.5)
    return pl.pallas_call(
        kern,
        out_shape=jax.ShapeDtypeStruct((t, n), BF16),
        grid=(t // tm, n // tn),
        in_specs=[pl.BlockSpec((tm, d), lambda i, j: (i, 0)),
                  pl.BlockSpec((d, tn), lambda i, j: (0, j))],
        out_specs=pl.BlockSpec((tm, tn), lambda i, j: (i, j)),
        compiler_params=_cparams(("parallel", "arbitrary")),
        name="qkv_proj",
    )(x, w_bf16)


def _glu_kernel(x_ref, wa_ref, wb_ref, ba_ref, bb_ref, o_ref):
    xb = x_ref[...].astype(BF16)
    a = jnp.dot(xb, wa_ref[...], preferred_element_type=F32) + ba_ref[...]
    b = jnp.dot(xb, wb_ref[...], preferred_element_type=F32) + bb_ref[...]
    o_ref[...] = a * jax.nn.sigmoid(b)


def _pw1_glu(x, w_bf16, bias):
    tm, tn = 1024, 512
    t, d = x.shape
    nb = d // tn
    bias2 = bias.reshape(1, 2 * d)
    return pl.pallas_call(
        _glu_kernel,
        out_shape=jax.ShapeDtypeStruct((t, d), F32),
        grid=(t // tm, nb),
        in_specs=[pl.BlockSpec((tm, d), lambda i, j: (i, 0)),
                  pl.BlockSpec((d, tn), lambda i, j: (0, j)),
                  pl.BlockSpec((d, tn), lambda i, j: (0, j + nb)),
                  pl.BlockSpec((1, tn), lambda i, j: (0, j)),
                  pl.BlockSpec((1, tn), lambda i, j: (0, j + nb))],
        out_specs=pl.BlockSpec((tm, tn), lambda i, j: (i, j)),
        compiler_params=_cparams(("parallel", "arbitrary")),
        name="pw1_glu",
    )(x, w_bf16, w_bf16, bias2, bias2)


N_SLOPE_SPLIT = 3
KPOS_LO = 128
HEADS_PER_STEP = 2
CHUNK_BLOCKS = 2
BIAS_LANE0 = N_BLOCKS
N_BIAS_LANES = 2 * N_SLOPE_SPLIT


def _moba_key_table():
    kpos = jnp.arange(SEQ, dtype=jnp.int32)
    onehot = (kpos[:, None] // MOBA_BLOCK == jnp.arange(N_BLOCKS)[None, :]).astype(BF16)
    hi = ((kpos // KPOS_LO) * KPOS_LO).astype(BF16)[:, None]
    lo = (kpos % KPOS_LO).astype(BF16)[:, None]
    pad = jnp.zeros((SEQ, LANES - N_BLOCKS - N_BIAS_LANES), BF16)
    return jnp.concatenate([onehot] + [hi] * N_SLOPE_SPLIT + [lo] * N_SLOPE_SPLIT + [pad], axis=1)


def _moba_slope_table():
    slopes = 2.0 ** (-8.0 * jnp.arange(1, N_HEADS + 1, dtype=F32) / N_HEADS)
    pieces = []
    rem = slopes
    for _ in range(N_SLOPE_SPLIT):
        p = _round_to_bf16(rem)
        pieces.append(p[:, None])
        rem = rem - p
    tab = jnp.concatenate([jnp.zeros((N_HEADS, BIAS_LANE0), F32)] + pieces + pieces
                          + [jnp.zeros((N_HEADS, LANES - BIAS_LANE0 - N_BIAS_LANES), F32)], axis=1)
    return tab.reshape(N_HEADS // HEADS_PER_STEP, HEADS_PER_STEP, LANES)


def _moba_kernel(q_ref, k_ref, v_ref, c_ref, sl_ref, o_ref, kmean_ref, qa_ref, m_ref, l_ref, acc_ref,
                 s0_ref, s1_ref):
    blk = MOBA_BLOCK
    hp_n = HEADS_PER_STEP
    own = pl.program_id(1)
    nt = (((1,), (1,)), ((), ()))
    hcols = lambda hp: slice(hp * HEAD_DIM, (hp + 1) * HEAD_DIM)

    @pl.when(own == 0)
    def _():
        kmean_ref[...] = jnp.zeros_like(kmean_ref)

        def body(j, carry):
            kb = k_ref[pl.ds(pl.multiple_of(j * blk, blk), blk), :].astype(F32)
            mean = jnp.mean(kb, axis=0, keepdims=True)
            for hp in range(hp_n):
                kmean_ref[hp, pl.ds(j, 1), :] = mean[:, hcols(hp)]
            return carry

        lax.fori_loop(0, N_BLOCKS, body, 0)

    lane = lax.broadcasted_iota(jnp.int32, (blk, LANES), 1)
    minus_inf = jnp.float32(-jnp.inf)
    own_rows = pl.ds(pl.multiple_of(own * blk, blk), blk)
    r_i = lax.broadcasted_iota(jnp.int32, (blk, blk), 0)
    c_i = lax.broadcasted_iota(jnp.int32, (blk, blk), 1)
    for hp in range(hp_n):
        q = q_ref[:, hcols(hp)]
        km = kmean_ref[hp]
        km_hi = km.astype(BF16)
        km_lo = (km - km_hi.astype(F32)).astype(BF16)
        gate = (lax.dot_general(q, km_hi, nt, preferred_element_type=F32)
                + lax.dot_general(q, km_lo, nt, preferred_element_type=F32))
        gm = jnp.where(lane < own, gate, minus_inf)
        sel = lane < 0
        for r in range(MOBA_TOPK):
            mx = jnp.max(gm, axis=1, keepdims=True)
            idx = jnp.min(jnp.where(gm == mx, lane, LANES), axis=1, keepdims=True)
            hit = lane == idx
            sel = sel | (hit & (r < own))
            gm = jnp.where(hit, minus_inf, gm)
        slope_lanes = jnp.broadcast_to(sl_ref[pl.ds(hp, 1), :], (blk, LANES)).astype(BF16)
        aug = jnp.where(lane < N_BLOCKS, jnp.where(sel, 0.0, NEG_INF).astype(BF16), slope_lanes)
        qa_ref[hp] = jnp.concatenate([q, aug], axis=1)

        aug_own = jnp.where(lane < N_BLOCKS, jnp.zeros_like(slope_lanes), slope_lanes)
        k_aug = jnp.concatenate([k_ref[own_rows, hcols(hp)], c_ref[own_rows, :]], axis=1)
        s = lax.dot_general(jnp.concatenate([q, aug_own], axis=1), k_aug, nt, preferred_element_type=F32)
        s = jnp.where(r_i >= c_i, s, NEG_INF)
        m0 = jnp.max(s, axis=1, keepdims=True)
        p = jnp.exp(s - m0)
        m_ref[hp] = jnp.broadcast_to(m0, (blk, LANES))
        l_ref[hp] = jnp.broadcast_to(jnp.sum(p, axis=1, keepdims=True), (blk, LANES))
        acc_ref[hp] = jnp.dot(p.astype(BF16), v_ref[own_rows, hcols(hp)], preferred_element_type=F32)

    width = CHUNK_BLOCKS * blk
    last_chunk = SEQ // width - 1

    def chunk_rows(c):
        return pl.ds(pl.multiple_of(c * width, width), width)

    def score_chunk(c, s_ref):
        rows = chunk_rows(c)
        c_rows = c_ref[rows, :]
        for hp in range(hp_n):
            k_aug = jnp.concatenate([k_ref[rows, hcols(hp)], c_rows], axis=1)
            s_ref[hp] = lax.dot_general(qa_ref[hp], k_aug, nt, preferred_element_type=F32)

    def absorb_chunk(c, s_ref):
        rows = chunk_rows(c)
        for hp in range(hp_n):
            s = s_ref[hp]
            m_old = m_ref[hp]
            m_new = jnp.maximum(m_old, jnp.max(s, axis=1, keepdims=True))
            alpha = jnp.exp(m_old - m_new)
            p = jnp.exp(s - jnp.tile(m_new, (1, width // LANES)))
            l_ref[hp] = alpha * l_ref[hp] + jnp.sum(p, axis=1, keepdims=True)
            acc_ref[hp] = alpha * acc_ref[hp] + jnp.dot(p.astype(BF16), v_ref[rows, hcols(hp)],
                                                        preferred_element_type=F32)
            m_ref[hp] = m_new

    score_chunk(0, s0_ref)

    def body(i, carry):
        c0 = 2 * i
        score_chunk(c0 + 1, s1_ref)
        absorb_chunk(c0, s0_ref)
        score_chunk(jnp.minimum(c0 + 2, last_chunk), s0_ref)
        absorb_chunk(c0 + 1, s1_ref)
        return carry

    n_pairs = lax.div(own + (2 * CHUNK_BLOCKS - 1), jnp.int32(2 * CHUNK_BLOCKS))
    lax.fori_loop(0, n_pairs, body, 0)
    for hp in range(hp_n):
        o_ref[:, hcols(hp)] = (acc_ref[hp] / l_ref[hp]).astype(o_ref.dtype)


def _moba_attention(qkv, key_table, slope_table):
    blk = MOBA_BLOCK
    hp_n = HEADS_PER_STEP
    w = hp_n * HEAD_DIM
    n_groups = N_HEADS // hp_n
    return pl.pallas_call(
        _moba_kernel,
        out_shape=jax.ShapeDtypeStruct((SEQ, D_MODEL), BF16),
        grid=(n_groups, N_BLOCKS),
        in_specs=[pl.BlockSpec((blk, w), lambda h, i: (i, h)),
                  pl.BlockSpec((SEQ, w), lambda h, i: (0, n_groups + h)),
                  pl.BlockSpec((SEQ, w), lambda h, i: (0, 2 * n_groups + h)),
                  pl.BlockSpec((SEQ, LANES), lambda h, i: (0, 0)),
                  pl.BlockSpec((None, hp_n, LANES), lambda h, i: (h, 0, 0))],
        out_specs=pl.BlockSpec((blk, w), lambda h, i: (i, h)),
        scratch_shapes=[pltpu.VMEM((hp_n, LANES, HEAD_DIM), F32),
                        pltpu.VMEM((hp_n, blk, 2 * LANES), BF16),
                        pltpu.VMEM((hp_n, blk, LANES), F32),
                        pltpu.VMEM((hp_n, blk, LANES), F32),
                        pltpu.VMEM((hp_n, blk, HEAD_DIM), F32),
                        pltpu.VMEM((hp_n, blk, CHUNK_BLOCKS * blk), F32),
                        pltpu.VMEM((hp_n, blk, CHUNK_BLOCKS * blk), F32)],
        compiler_params=_cparams(("arbitrary", "arbitrary")),
        name="moba_attention",
    )(qkv, qkv, qkv, key_table, slope_table)


def _dwconv_kernel(cur_ref, prev_ref, w_ref, b_ref, o_ref, win_ref, sh_ref, *, row_chunk):
    tm = cur_ref.shape[0]
    keep = (pl.program_id(0) > 0).astype(F32)
    win_ref[pl.ds(0, CONV_HALO), :] = prev_ref[...] * keep
    win_ref[pl.ds(CONV_HALO, tm), :] = cur_ref[...]
    first = CONV_HALO - (CONV_WIDTH - 1)
    n_sh = sh_ref.shape[1]
    for b in range(1, SUBLANES):
        sh_ref[b - 1] = win_ref[pl.ds(b, n_sh), :]
    for rc in range(tm // row_chunk):
        acc = jnp.broadcast_to(b_ref[...], (row_chunk, cur_ref.shape[1]))
        for k in range(CONV_WIDTH):
            b = (first + k) % SUBLANES
            a = first + k - b + rc * row_chunk
            src = win_ref if b == 0 else sh_ref.at[b - 1]
            acc = acc + w_ref[pl.ds(k, 1), :] * src[pl.ds(a, row_chunk), :]
        o_ref[pl.ds(rc * row_chunk, row_chunk), :] = acc


def _dwconv(g, w_dw, b_dw):
    tm, tc = 512, 512
    t, d = g.shape
    w_pad = jnp.concatenate([w_dw, jnp.zeros((CONV_HALO - CONV_WIDTH, d), F32)], axis=0)
    halo_per_tile = tm // CONV_HALO
    kern = functools.partial(_dwconv_kernel, row_chunk=64)
    return pl.pallas_call(
        kern,
        out_shape=jax.ShapeDtypeStruct((t, d), F32),
        grid=(t // tm, d // tc),
        in_specs=[pl.BlockSpec((tm, tc), lambda i, j: (i, j)),
                  pl.BlockSpec((CONV_HALO, tc), lambda i, j: (jnp.maximum(i * halo_per_tile - 1, 0), j)),
                  pl.BlockSpec((CONV_HALO, tc), lambda i, j: (0, j)),
                  pl.BlockSpec((1, tc), lambda i, j: (0, j))],
        out_specs=pl.BlockSpec((tm, tc), lambda i, j: (i, j)),
        scratch_shapes=[pltpu.VMEM((tm + CONV_HALO, tc), F32),
                        pltpu.VMEM((SUBLANES - 1, tm + CONV_HALO - SUBLANES, tc), F32)],
        compiler_params=_cparams(("parallel", "arbitrary")),
        name="dwconv",
    )(g, g, w_pad, b_dw.reshape(1, d))


def _layer_norm(z, g, b):
    mu = jnp.mean(z, axis=-1, keepdims=True)
    zc = z - mu
    var = jnp.mean(zc * zc, axis=-1, keepdims=True)
    return zc * lax.rsqrt(var + LN_EPS) * g + b


def _route_kernel(xres_ref, a_ref, w_ref, png_ref, pnb_ref, lng_ref, lnb_ref,
                  wr_hi_ref, wr_lo_ref, br_ref,
                  x1_ref, meta_ref, cnt_ref, carry_ref, *, pre_norm):
    tm = xres_ref.shape[0]

    @pl.when(pl.program_id(0) == 0)
    def _():
        carry_ref[...] = jnp.zeros_like(carry_ref)

    a = a_ref[...]
    if pre_norm:
        a = jax.nn.silu(_layer_norm(a.astype(F32), png_ref[...], pnb_ref[...]))
    y = jnp.dot(a.astype(BF16), w_ref[...], preferred_element_type=F32)
    x1 = _layer_norm(DEEPNORM_ALPHA * xres_ref[...] + y, lng_ref[...], lnb_ref[...])
    x1_ref[...] = x1

    x_hi = x1.astype(BF16)
    x_lo = (x1 - x_hi.astype(F32)).astype(BF16)
    logits = (jnp.dot(x_hi, wr_hi_ref[...], preferred_element_type=F32)
              + jnp.dot(x_lo, wr_hi_ref[...], preferred_element_type=F32)
              + jnp.dot(x_hi, wr_lo_ref[...], preferred_element_type=F32)
              + br_ref[...])
    lane = lax.broadcasted_iota(jnp.int32, logits.shape, 1)
    ninf = jnp.float32(-jnp.inf)
    gl = jnp.where(lane < N_GROUPS, logits, ninf)
    gmax = jnp.max(gl, axis=1, keepdims=True)
    g_sel = jnp.min(jnp.where(gl == gmax, lane, LANES), axis=1, keepdims=True)
    g_p = 1.0 / jnp.sum(jnp.exp(gl - gmax), axis=1, keepdims=True)
    lo = N_GROUPS + EXPERTS_PER_GROUP * g_sel
    el = jnp.where((lane >= lo) & (lane < lo + EXPERTS_PER_GROUP), logits, ninf)
    v1 = jnp.max(el, axis=1, keepdims=True)
    i1 = jnp.min(jnp.where(el == v1, lane, LANES), axis=1, keepdims=True)
    el2 = jnp.where(lane == i1, ninf, el)
    v2 = jnp.max(el2, axis=1, keepdims=True)
    i2 = jnp.min(jnp.where(el2 == v2, lane, LANES), axis=1, keepdims=True)
    t = jnp.exp(v2 - v1)
    w1 = g_p / (1.0 + t)
    w2 = g_p * t / (1.0 + t)
    e1 = i1 - N_GROUPS
    e2 = i2 - N_GROUPS

    onehot = ((lane == e1) | (lane == e2)).astype(BF16)
    r_i = lax.broadcasted_iota(jnp.int32, (tm, tm), 0)
    c_i = lax.broadcasted_iota(jnp.int32, (tm, tm), 1)
    lower = (c_i < r_i).astype(BF16)
    before = jnp.dot(lower, onehot, preferred_element_type=F32) + carry_ref[...]
    rank1 = jnp.sum(jnp.where(lane == e1, before, 0.0), axis=1, keepdims=True)
    rank2 = jnp.sum(jnp.where(lane == e2, before, 0.0), axis=1, keepdims=True)
    carry_ref[...] = carry_ref[...] + jnp.sum(onehot.astype(F32), axis=0, keepdims=True)
    cnt_ref[...] = carry_ref[...]

    meta = jnp.where(lane == 0, e1.astype(F32),
           jnp.where(lane == 1, e2.astype(F32),
           jnp.where(lane == 2, w1,
           jnp.where(lane == 3, w2,
           jnp.where(lane == 4, rank1,
           jnp.where(lane == 5, rank2, 0.0))))))
    meta_ref[...] = meta


def _proj_norm_route(xres, a, w_bf16, pn_g, pn_b, ln_g, ln_b, wr_hi, wr_lo, br, *, pre_norm):
    tm = TM_ROUTE
    t, d = xres.shape
    row = lambda i: (i, 0)
    fix = lambda i: (0, 0)
    vec = lambda v: v.reshape(1, d)
    kern = functools.partial(_route_kernel, pre_norm=pre_norm)
    return pl.pallas_call(
        kern,
        out_shape=(jax.ShapeDtypeStruct((t, d), F32),
                   jax.ShapeDtypeStruct((t, LANES), F32),
                   jax.ShapeDtypeStruct((1, LANES), F32)),
        grid=(t // tm,),
        in_specs=[pl.BlockSpec((tm, d), row),
                  pl.BlockSpec((tm, d), row),
                  pl.BlockSpec((d, d), fix),
                  pl.BlockSpec((1, d), fix), pl.BlockSpec((1, d), fix),
                  pl.BlockSpec((1, d), fix), pl.BlockSpec((1, d), fix),
                  pl.BlockSpec((d, LANES), fix), pl.BlockSpec((d, LANES), fix),
                  pl.BlockSpec((1, LANES), fix)],
        out_specs=(pl.BlockSpec((tm, d), row),
                   pl.BlockSpec((tm, LANES), row),
                   pl.BlockSpec((1, LANES), fix)),
        scratch_shapes=[pltpu.VMEM((1, LANES), F32)],
        compiler_params=_cparams(("arbitrary",)),
        name="proj_norm_route",
    )(xres, a, w_bf16, vec(pn_g), vec(pn_b), vec(ln_g), vec(ln_b), wr_hi, wr_lo, br)


def _row_copy(src_ref, src_row, dst_ref, dst_row, sem):
    return pltpu.make_async_copy(src_ref.at[pl.ds(src_row, 1), :], dst_ref.at[pl.ds(dst_row, 1), :], sem)


def _scatter_kernel(pos_ref, x_ref, xs_in_ref, xs_ref, sem):
    del xs_in_ref
    tm = x_ref.shape[0]
    base = 2 * pl.program_id(0) * tm

    def start(tok, carry):
        _row_copy(x_ref, tok, xs_ref, pos_ref[base + 2 * tok], sem).start()
        _row_copy(x_ref, tok, xs_ref, pos_ref[base + 2 * tok + 1], sem).start()
        return carry

    def wait(tok, carry):
        _row_copy(x_ref, tok, xs_ref, pos_ref[base + 2 * tok], sem).wait()
        _row_copy(x_ref, tok, xs_ref, pos_ref[base + 2 * tok + 1], sem).wait()
        return carry

    lax.fori_loop(0, tm, start, 0, unroll=DMA_LOOP_UNROLL)
    lax.fori_loop(0, tm, wait, 0, unroll=DMA_LOOP_UNROLL)


def _scatter_rows(pos, x1):
    tm = TM_MOVE
    t, d = x1.shape
    xs0 = jnp.zeros((N_SORTED_ROWS, d), F32)
    return pl.pallas_call(
        _scatter_kernel,
        out_shape=jax.ShapeDtypeStruct((N_SORTED_ROWS, d), F32),
        grid_spec=pltpu.PrefetchScalarGridSpec(
            num_scalar_prefetch=1,
            grid=(t // tm,),
            in_specs=[pl.BlockSpec((tm, d), lambda i, *_: (i, 0)),
                      pl.BlockSpec(memory_space=pl.ANY)],
            out_specs=pl.BlockSpec(memory_space=pl.ANY),
            scratch_shapes=[pltpu.SemaphoreType.DMA]),
        input_output_aliases={2: 0},
        compiler_params=_cparams(("arbitrary",)),
        name="moe_scatter",
    )(pos, x1, xs0)


def _expert_kernel(te_ref, nu_ref, xs_ref, wg_ref, wu_ref, wd_ref, y_ref, wg_bf, wu_bf, wd_bf):
    i = pl.program_id(0)
    used = i < nu_ref[0]
    new_expert = (i == 0) | (te_ref[i] != te_ref[jnp.maximum(i - 1, 0)])

    @pl.when(used & new_expert)
    def _():
        wg_bf[...] = wg_ref[...].astype(BF16)
        wu_bf[...] = wu_ref[...].astype(BF16)
        wd_bf[...] = wd_ref[...].astype(BF16)

    @pl.when(used)
    def _():
        xb = xs_ref[...].astype(BF16)
        g = jnp.dot(xb, wg_bf[...], preferred_element_type=F32)
        u = jnp.dot(xb, wu_bf[...], preferred_element_type=F32)
        h = (jax.nn.silu(g) * u).astype(BF16)
        y_ref[...] = jnp.dot(h, wd_bf[...], preferred_element_type=F32)

    @pl.when(jnp.logical_not(used))
    def _():
        y_ref[...] = jnp.zeros_like(y_ref)


def _expert_mlp(tile_expert, n_used, xs, wg, wu, wd):
    tm = TM_EXPERT
    d, f = D_MODEL, D_EXPERT
    return pl.pallas_call(
        _expert_kernel,
        out_shape=jax.ShapeDtypeStruct((N_SORTED_ROWS, d), F32),
        grid_spec=pltpu.PrefetchScalarGridSpec(
            num_scalar_prefetch=2,
            grid=(N_EXPERT_TILES,),
            in_specs=[pl.BlockSpec((tm, d), lambda i, te, nu: (i, 0)),
                      pl.BlockSpec((None, d, f), lambda i, te, nu: (te[i], 0, 0)),
                      pl.BlockSpec((None, d, f), lambda i, te, nu: (te[i], 0, 0)),
                      pl.BlockSpec((None, f, d), lambda i, te, nu: (te[i], 0, 0))],
            out_specs=pl.BlockSpec((tm, d), lambda i, te, nu: (i, 0)),
            scratch_shapes=[pltpu.VMEM((d, f), BF16), pltpu.VMEM((d, f), BF16), pltpu.VMEM((f, d), BF16)]),
        compiler_params=_cparams(("arbitrary",)),
        name="moe_experts",
    )(tile_expert, n_used, xs, wg, wu, wd)


def _combine_kernel(pos_ref, x_ref, meta_ref, y_ref, lng_ref, lnb_ref, o_ref, ybuf_ref, sem):
    tm = x_ref.shape[0]
    base = 2 * pl.program_id(0) * tm

    def start(tok, carry):
        _row_copy(y_ref, pos_ref[base + 2 * tok], ybuf_ref.at[0], tok, sem).start()
        _row_copy(y_ref, pos_ref[base + 2 * tok + 1], ybuf_ref.at[1], tok, sem).start()
        return carry

    def wait(tok, carry):
        _row_copy(y_ref, pos_ref[base + 2 * tok], ybuf_ref.at[0], tok, sem).wait()
        _row_copy(y_ref, pos_ref[base + 2 * tok + 1], ybuf_ref.at[1], tok, sem).wait()
        return carry

    lax.fori_loop(0, tm, start, 0, unroll=DMA_LOOP_UNROLL)
    lax.fori_loop(0, tm, wait, 0, unroll=DMA_LOOP_UNROLL)
    meta = meta_ref[...]
    lane = lax.broadcasted_iota(jnp.int32, meta.shape, 1)
    w1 = jnp.sum(jnp.where(lane == 2, meta, 0.0), axis=1, keepdims=True)
    w2 = jnp.sum(jnp.where(lane == 3, meta, 0.0), axis=1, keepdims=True)
    y = w1 * ybuf_ref[0] + w2 * ybuf_ref[1]
    o_ref[...] = _layer_norm(DEEPNORM_ALPHA * x_ref[...] + y, lng_ref[...], lnb_ref[...])


def _combine_norm(pos, x1, meta, y, ln_g, ln_b):
    tm = TM_MOVE
    t, d = x1.shape
    return pl.pallas_call(
        _combine_kernel,
        out_shape=jax.ShapeDtypeStruct((t, d), F32),
        grid_spec=pltpu.PrefetchScalarGridSpec(
            num_scalar_prefetch=1,
            grid=(t // tm,),
            in_specs=[pl.BlockSpec((tm, d), lambda i, *_: (i, 0)),
                      pl.BlockSpec((tm, LANES), lambda i, *_: (i, 0)),
                      pl.BlockSpec(memory_space=pl.ANY),
                      pl.BlockSpec((1, d), lambda i, *_: (0, 0)),
                      pl.BlockSpec((1, d), lambda i, *_: (0, 0))],
            out_specs=pl.BlockSpec((tm, d), lambda i, *_: (i, 0)),
            scratch_shapes=[pltpu.VMEM((2, tm, d), F32), pltpu.SemaphoreType.DMA]),
        compiler_params=_cparams(("arbitrary",)),
        name="moe_combine_norm",
    )(pos, x1, meta, y, ln_g.reshape(1, d), ln_b.reshape(1, d))


def _moe_layer(x1, meta, counts, wg, wu, wd, ln_g, ln_b):
    t = x1.shape[0]
    e12 = meta[:, 0:2].astype(jnp.int32).reshape(2 * t)
    r12 = meta[:, 4:6].astype(jnp.int32).reshape(2 * t)
    cnt = counts[0, :N_EXPERTS].astype(jnp.int32)
    padded = ((cnt + TM_EXPERT - 1) // TM_EXPERT) * TM_EXPERT
    ends = jnp.cumsum(padded)
    off = (ends - padded).astype(jnp.int32)
    pos = off[e12] + r12
    tile_start = jnp.arange(N_EXPERT_TILES, dtype=jnp.int32) * TM_EXPERT
    tile_expert = jnp.sum((tile_start[:, None] >= ends[None, :]).astype(jnp.int32), axis=1)
    tile_expert = jnp.minimum(tile_expert, N_EXPERTS - 1)
    n_used = (ends[-1:] // TM_EXPERT).astype(jnp.int32)
    xs = _scatter_rows(pos, x1)
    y = _expert_mlp(tile_expert, n_used, xs, wg, wu, wd)
    return _combine_norm(pos, x1, meta, y, ln_g, ln_b)


def _router_weights(w_grp, b_grp, w_rt, b_rt):
    d = w_grp.shape[0]
    pad = LANES - N_GROUPS - N_EXPERTS
    wr = jnp.concatenate([w_grp, w_rt.reshape(d, N_EXPERTS), jnp.zeros((d, pad), F32)], axis=1)
    br = jnp.concatenate([b_grp, b_rt.reshape(N_EXPERTS), jnp.zeros((pad,), F32)]).reshape(1, LANES)
    wr_hi = _round_to_bf16(wr)
    wr_lo = _round_to_bf16(wr - wr_hi)
    return wr_hi.astype(BF16), wr_lo.astype(BF16), br


def kernel(x, attn_w_qkv, attn_w_o, conv_w_pw1, conv_b_pw1, conv_w_dw, conv_b_dw, conv_ln_g, conv_ln_b,
           conv_w_pw2, mix_ln_g, mix_ln_b, moe_w_grp, moe_b_grp, moe_w_rt, moe_b_rt, moe_w_gate, moe_w_up,
           moe_w_down, ffn_ln_g, ffn_ln_b):
    b, s, d = x.shape
    assert (b, s, d) == (1, SEQ, D_MODEL)
    xt = x.reshape(s, d)
    ones = jnp.ones((d,), F32)
    zeros = jnp.zeros((d,), F32)

    def moe_weights(i):
        shp_in = (N_EXPERTS, D_MODEL, D_EXPERT)
        return (moe_w_gate[i].reshape(shp_in), moe_w_up[i].reshape(shp_in),
                moe_w_down[i].reshape(N_EXPERTS, D_EXPERT, D_MODEL))

    qkv = _qkv_proj(xt, attn_w_qkv[0].astype(BF16))
    o = _moba_attention(qkv, _moba_key_table(), _moba_slope_table())
    wr_hi, wr_lo, br = _router_weights(moe_w_grp[0], moe_b_grp[0], moe_w_rt[0], moe_b_rt[0])
    x1, meta, counts = _proj_norm_route(xt, o, attn_w_o[0].astype(BF16), ones, zeros, mix_ln_g[0], mix_ln_b[0],
                                        wr_hi, wr_lo, br, pre_norm=False)
    x2 = _moe_layer(x1, meta, counts, *moe_weights(0), ffn_ln_g[0], ffn_ln_b[0])

    g = _pw1_glu(x2, conv_w_pw1[0].astype(BF16), conv_b_pw1[0])
    c = _dwconv(g, conv_w_dw[0], conv_b_dw[0])
    wr_hi, wr_lo, br = _router_weights(moe_w_grp[1], moe_b_grp[1], moe_w_rt[1], moe_b_rt[1])
    x3, meta, counts = _proj_norm_route(x2, c, conv_w_pw2[0].astype(BF16), conv_ln_g[0], conv_ln_b[0],
                                        mix_ln_g[1], mix_ln_b[1], wr_hi, wr_lo, br, pre_norm=True)
    x4 = _moe_layer(x3, meta, counts, *moe_weights(1), ffn_ln_g[1], ffn_ln_b[1])
    return x4.reshape(b, s, d)
```

```python
import functools

import jax
import jax.numpy as jnp
from jax import lax
from jax.experimental import pallas as pl
from jax.experimental.pallas import tpu as pltpu

F32 = jnp.float32
BF16 = jnp.bfloat16

D_MODEL = 2048
SEQ = 16384
DEPTH = 2
N_HEADS = 16
HEAD_DIM = D_MODEL // N_HEADS
MOBA_BLOCK = 256
MOBA_TOPK = 3
N_BLOCKS = SEQ // MOBA_BLOCK
CONV_WIDTH = 31
N_GROUPS = 4
EXPERTS_PER_GROUP = 8
N_EXPERTS = N_GROUPS * EXPERTS_PER_GROUP
D_EXPERT = D_MODEL // 4
LN_EPS = 1e-5
NEG_INF = -1e30
DEEPNORM_ALPHA = (2.0 * DEPTH) ** 0.25

LANES = 128
SUBLANES = 8
CONV_HALO = 32
TM_ROUTE = 512
TM_MOVE = 256
DMA_LOOP_UNROLL = 8
TM_EXPERT = 256
N_EXPERT_TILES = (2 * SEQ) // TM_EXPERT + N_EXPERTS
N_SORTED_ROWS = N_EXPERT_TILES * TM_EXPERT
VMEM_LIMIT = 56 * 1024 * 1024


def _round_to_bf16(v):
    return lax.reduce_precision(v, exponent_bits=8, mantissa_bits=7)


def _cparams(sem):
    return pltpu.CompilerParams(dimension_semantics=sem, vmem_limit_bytes=VMEM_LIMIT)


def _qkv_kernel(x_ref, w_ref, o_ref, *, n_scaled_tiles, scale):
    acc = jnp.dot(x_ref[...].astype(BF16), w_ref[...], preferred_element_type=F32)
    s = jnp.where(pl.program_id(1) < n_scaled_tiles, scale, 1.0).astype(F32)
    o_ref[...] = (acc * s).astype(o_ref.dtype)


def _qkv_proj(x, w_bf16):
    tm, tn = 1024, 1024
    t, d = x.shape
    n = w_bf16.shape[1]
    kern = functools.partial(_qkv_kernel, n_scaled_tiles=D_MODEL // tn, scale=HEAD_DIM ** -0.5)
    return pl.pallas_call(
        kern,
        out_shape=jax.ShapeDtypeStruct((t, n), BF16),
        grid=(t // tm, n // tn),
        in_specs=[pl.BlockSpec((tm, d), lambda i, j: (i, 0)),
                  pl.BlockSpec((d, tn), lambda i, j: (0, j))],
        out_specs=pl.BlockSpec((tm, tn), lambda i, j: (i, j)),
        compiler_params=_cparams(("parallel", "arbitrary")),
        name="qkv_proj",
    )(x, w_bf16)


def _glu_kernel(x_ref, wa_ref, wb_ref, ba_ref, bb_ref, o_ref):
    xb = x_ref[...].astype(BF16)
    a = jnp.dot(xb, wa_ref[...], preferred_element_type=F32) + ba_ref[...]
    b = jnp.dot(xb, wb_ref[...], preferred_element_type=F32) + bb_ref[...]
    o_ref[...] = a * jax.nn.sigmoid(b)


def _pw1_glu(x, w_bf16, bias):
    tm, tn = 1024, 512
    t, d = x.shape
    nb = d // tn
    bias2 = bias.reshape(1, 2 * d)
    return pl.pallas_call(
        _glu_kernel,
        out_shape=jax.ShapeDtypeStruct((t, d), F32),
        grid=(t // tm, nb),
        in_specs=[pl.BlockSpec((tm, d), lambda i, j: (i, 0)),
                  pl.BlockSpec((d, tn), lambda i, j: (0, j)),
                  pl.BlockSpec((d, tn), lambda i, j: (0, j + nb)),
                  pl.BlockSpec((1, tn), lambda i, j: (0, j)),
                  pl.BlockSpec((1, tn), lambda i, j: (0, j + nb))],
        out_specs=pl.BlockSpec((tm, tn), lambda i, j: (i, j)),
        compiler_params=_cparams(("parallel", "arbitrary")),
        name="pw1_glu",
    )(x, w_bf16, w_bf16, bias2, bias2)


N_SLOPE_SPLIT = 3
KPOS_LO = 128
HEADS_PER_STEP = 2
CHUNK_BLOCKS = 2
BIAS_LANE0 = N_BLOCKS
N_BIAS_LANES = 2 * N_SLOPE_SPLIT


def _moba_key_table():
    kpos = jnp.arange(SEQ, dtype=jnp.int32)
    onehot = (kpos[:, None] // MOBA_BLOCK == jnp.arange(N_BLOCKS)[None, :]).astype(BF16)
    hi = ((kpos // KPOS_LO) * KPOS_LO).astype(BF16)[:, None]
    lo = (kpos % KPOS_LO).astype(BF16)[:, None]
    pad = jnp.zeros((SEQ, LANES - N_BLOCKS - N_BIAS_LANES), BF16)
    return jnp.concatenate([onehot] + [hi] * N_SLOPE_SPLIT + [lo] * N_SLOPE_SPLIT + [pad], axis=1)


def _moba_slope_table():
    slopes = 2.0 ** (-8.0 * jnp.arange(1, N_HEADS + 1, dtype=F32) / N_HEADS)
    pieces = []
    rem = slopes
    for _ in range(N_SLOPE_SPLIT):
        p = _round_to_bf16(rem)
        pieces.append(p[:, None])
        rem = rem - p
    tab = jnp.concatenate([jnp.zeros((N_HEADS, BIAS_LANE0), F32)] + pieces + pieces
                          + [jnp.zeros((N_HEADS, LANES - BIAS_LANE0 - N_BIAS_LANES), F32)], axis=1)
    return tab.reshape(N_HEADS // HEADS_PER_STEP, HEADS_PER_STEP, LANES)


def _moba_kernel(q_ref, k_ref, vt_ref, c_ref, sl_ref, o_ref, kmean_ref, qa_ref, m_ref, l_ref, acc_ref,
                 s0_ref, s1_ref):
    blk = MOBA_BLOCK
    hp_n = HEADS_PER_STEP
    own = pl.program_id(1)
    nt = (((1,), (1,)), ((), ()))
    hcols = lambda hp: slice(hp * HEAD_DIM, (hp + 1) * HEAD_DIM)

    @pl.when(own == 0)
    def _():
        kmean_ref[...] = jnp.zeros_like(kmean_ref)

        def body(j, carry):
            kb = k_ref[pl.ds(pl.multiple_of(j * blk, blk), blk), :].astype(F32)
            mean = jnp.mean(kb, axis=0, keepdims=True)
            for hp in range(hp_n):
                kmean_ref[hp, pl.ds(j, 1), :] = mean[:, hcols(hp)]
            return carry

        lax.fori_loop(0, N_BLOCKS, body, 0)

    lane = lax.broadcasted_iota(jnp.int32, (blk, LANES), 1)
    minus_inf = jnp.float32(-jnp.inf)
    own_rows = pl.ds(pl.multiple_of(own * blk, blk), blk)
    key_i = lax.broadcasted_iota(jnp.int32, (blk, blk), 0)
    qry_i = lax.broadcasted_iota(jnp.int32, (blk, blk), 1)
    hrows = hcols
    for hp in range(hp_n):
        q = q_ref[:, hcols(hp)]
        km = kmean_ref[hp]
        km_hi = km.astype(BF16)
        km_lo = (km - km_hi.astype(F32)).astype(BF16)
        gate = (lax.dot_general(q, km_hi, nt, preferred_element_type=F32)
                + lax.dot_general(q, km_lo, nt, preferred_element_type=F32))
        gm = jnp.where(lane < own, gate, minus_inf)
        sel = lane < 0
        for r in range(MOBA_TOPK):
            mx = jnp.max(gm, axis=1, keepdims=True)
            idx = jnp.min(jnp.where(gm == mx, lane, LANES), axis=1, keepdims=True)
            hit = lane == idx
            sel = sel | (hit & (r < own))
            gm = jnp.where(hit, minus_inf, gm)
        slope_lanes = jnp.broadcast_to(sl_ref[pl.ds(hp, 1), :], (blk, LANES)).astype(BF16)
        aug = jnp.where(lane < N_BLOCKS, jnp.where(sel, 0.0, NEG_INF).astype(BF16), slope_lanes)
        qa_ref[hp] = jnp.concatenate([q, aug], axis=1)

        aug_own = jnp.where(lane < N_BLOCKS, jnp.zeros_like(slope_lanes), slope_lanes)
        k_aug = jnp.concatenate([k_ref[own_rows, hcols(hp)], c_ref[own_rows, :]], axis=1)
        s = lax.dot_general(k_aug, jnp.concatenate([q, aug_own], axis=1), nt, preferred_element_type=F32)
        s = jnp.where(key_i <= qry_i, s, NEG_INF)
        m0 = jnp.max(s, axis=0, keepdims=True)
        p = jnp.exp(s - m0)
        m_ref[hp] = m0
        l_ref[hp] = jnp.sum(p, axis=0, keepdims=True)
        acc_ref[hp] = jnp.dot(vt_ref[own, hrows(hp), :], p.astype(BF16), preferred_element_type=F32)

    width = CHUNK_BLOCKS * blk
    last_chunk = SEQ // width - 1

    def score_chunk(c, s_ref):
        rows = pl.ds(pl.multiple_of(c * width, width), width)
        c_rows = c_ref[rows, :]
        for hp in range(hp_n):
            k_aug = jnp.concatenate([k_ref[rows, hcols(hp)], c_rows], axis=1)
            s_ref[hp] = lax.dot_general(k_aug, qa_ref[hp], nt, preferred_element_type=F32)

    def absorb_chunk(c, s_ref):
        for hp in range(hp_n):
            s = s_ref[hp]
            m_old = m_ref[hp]
            m_new = jnp.maximum(m_old, jnp.max(s, axis=0, keepdims=True))
            alpha = jnp.exp(m_old - m_new)
            p = jnp.exp(s - m_new)
            l_ref[hp] = alpha * l_ref[hp] + jnp.sum(p, axis=0, keepdims=True)
            p = p.astype(BF16)
            pv = jnp.dot(vt_ref[c * CHUNK_BLOCKS, hrows(hp), :], p[:blk], preferred_element_type=F32)
            for b in range(1, CHUNK_BLOCKS):
                pv = pv + jnp.dot(vt_ref[c * CHUNK_BLOCKS + b, hrows(hp), :], p[b * blk:(b + 1) * blk],
                                  preferred_element_type=F32)
            acc_ref[hp] = alpha * acc_ref[hp] + pv
            m_ref[hp] = m_new

    score_chunk(0, s0_ref)

    def body(i, carry):
        c0 = 2 * i
        score_chunk(c0 + 1, s1_ref)
        absorb_chunk(c0, s0_ref)
        score_chunk(jnp.minimum(c0 + 2, last_chunk), s0_ref)
        absorb_chunk(c0 + 1, s1_ref)
        return carry

    n_pairs = lax.div(own + (2 * CHUNK_BLOCKS - 1), jnp.int32(2 * CHUNK_BLOCKS))
    lax.fori_loop(0, n_pairs, body, 0)
    for hp in range(hp_n):
        o_ref[:, hcols(hp)] = (acc_ref[hp] / l_ref[hp]).T.astype(o_ref.dtype)


def _moba_attention(qkv, key_table, slope_table):
    blk = MOBA_BLOCK
    hp_n = HEADS_PER_STEP
    w = hp_n * HEAD_DIM
    n_groups = N_HEADS // hp_n
    v_t = qkv[:, 2 * D_MODEL:].reshape(N_BLOCKS, blk, D_MODEL).transpose(0, 2, 1)
    return pl.pallas_call(
        _moba_kernel,
        out_shape=jax.ShapeDtypeStruct((SEQ, D_MODEL), BF16),
        grid=(n_groups, N_BLOCKS),
        in_specs=[pl.BlockSpec((blk, w), lambda h, i: (i, h)),
                  pl.BlockSpec((SEQ, w), lambda h, i: (0, n_groups + h)),
                  pl.BlockSpec((N_BLOCKS, w, blk), lambda h, i: (0, h, 0)),
                  pl.BlockSpec((SEQ, LANES), lambda h, i: (0, 0)),
                  pl.BlockSpec((None, hp_n, LANES), lambda h, i: (h, 0, 0))],
        out_specs=pl.BlockSpec((blk, w), lambda h, i: (i, h)),
        scratch_shapes=[pltpu.VMEM((hp_n, LANES, HEAD_DIM), F32),
                        pltpu.VMEM((hp_n, blk, 2 * LANES), BF16),
                        pltpu.VMEM((hp_n, 1, blk), F32),
                        pltpu.VMEM((hp_n, 1, blk), F32),
                        pltpu.VMEM((hp_n, HEAD_DIM, blk), F32),
                        pltpu.VMEM((hp_n, CHUNK_BLOCKS * blk, blk), F32),
                        pltpu.VMEM((hp_n, CHUNK_BLOCKS * blk, blk), F32)],
        compiler_params=_cparams(("arbitrary", "arbitrary")),
        name="moba_attention",
    )(qkv, qkv, v_t, key_table, slope_table)


def _dwconv_kernel(cur_ref, prev_ref, w_ref, b_ref, o_ref, win_ref, sh_ref, *, row_chunk):
    tm = cur_ref.shape[0]
    keep = (pl.program_id(0) > 0).astype(F32)
    win_ref[pl.ds(0, CONV_HALO), :] = prev_ref[...] * keep
    win_ref[pl.ds(CONV_HALO, tm), :] = cur_ref[...]
    first = CONV_HALO - (CONV_WIDTH - 1)
    n_sh = sh_ref.shape[1]
    for b in range(1, SUBLANES):
        sh_ref[b - 1] = win_ref[pl.ds(b, n_sh), :]
    for rc in range(tm // row_chunk):
        acc = jnp.broadcast_to(b_ref[...], (row_chunk, cur_ref.shape[1]))
        for k in range(CONV_WIDTH):
            b = (first + k) % SUBLANES
            a = first + k - b + rc * row_chunk
            src = win_ref if b == 0 else sh_ref.at[b - 1]
            acc = acc + w_ref[pl.ds(k, 1), :] * src[pl.ds(a, row_chunk), :]
        o_ref[pl.ds(rc * row_chunk, row_chunk), :] = acc


def _dwconv(g, w_dw, b_dw):
    tm, tc = 512, 512
    t, d = g.shape
    w_pad = jnp.concatenate([w_dw, jnp.zeros((CONV_HALO - CONV_WIDTH, d), F32)], axis=0)
    halo_per_tile = tm // CONV_HALO
    kern = functools.partial(_dwconv_kernel, row_chunk=64)
    return pl.pallas_call(
        kern,
        out_shape=jax.ShapeDtypeStruct((t, d), F32),
        grid=(t // tm, d // tc),
        in_specs=[pl.BlockSpec((tm, tc), lambda i, j: (i, j)),
                  pl.BlockSpec((CONV_HALO, tc), lambda i, j: (jnp.maximum(i * halo_per_tile - 1, 0), j)),
                  pl.BlockSpec((CONV_HALO, tc), lambda i, j: (0, j)),
                  pl.BlockSpec((1, tc), lambda i, j: (0, j))],
        out_specs=pl.BlockSpec((tm, tc), lambda i, j: (i, j)),
        scratch_shapes=[pltpu.VMEM((tm + CONV_HALO, tc), F32),
                        pltpu.VMEM((SUBLANES - 1, tm + CONV_HALO - SUBLANES, tc), F32)],
        compiler_params=_cparams(("parallel", "arbitrary")),
        name="dwconv",
    )(g, g, w_pad, b_dw.reshape(1, d))


def _layer_norm(z, g, b):
    mu = jnp.mean(z, axis=-1, keepdims=True)
    zc = z - mu
    var = jnp.mean(zc * zc, axis=-1, keepdims=True)
    return zc * lax.rsqrt(var + LN_EPS) * g + b


def _route_kernel(xres_ref, a_ref, w_ref, png_ref, pnb_ref, lng_ref, lnb_ref,
                  wr_hi_ref, wr_lo_ref, br_ref,
                  x1_ref, meta_ref, cnt_ref, carry_ref, *, pre_norm):
    tm = xres_ref.shape[0]

    @pl.when(pl.program_id(0) == 0)
    def _():
        carry_ref[...] = jnp.zeros_like(carry_ref)

    a = a_ref[...]
    if pre_norm:
        a = jax.nn.silu(_layer_norm(a.astype(F32), png_ref[...], pnb_ref[...]))
    y = jnp.dot(a.astype(BF16), w_ref[...], preferred_element_type=F32)
    x1 = _layer_norm(DEEPNORM_ALPHA * xres_ref[...] + y, lng_ref[...], lnb_ref[...])
    x1_ref[...] = x1

    x_hi = x1.astype(BF16)
    x_lo = (x1 - x_hi.astype(F32)).astype(BF16)
    logits = (jnp.dot(x_hi, wr_hi_ref[...], preferred_element_type=F32)
              + jnp.dot(x_lo, wr_hi_ref[...], preferred_element_type=F32)
              + jnp.dot(x_hi, wr_lo_ref[...], preferred_element_type=F32)
              + br_ref[...])
    lane = lax.broadcasted_iota(jnp.int32, logits.shape, 1)
    ninf = jnp.float32(-jnp.inf)
    gl = jnp.where(lane < N_GROUPS, logits, ninf)
    gmax = jnp.max(gl, axis=1, keepdims=True)
    g_sel = jnp.min(jnp.where(gl == gmax, lane, LANES), axis=1, keepdims=True)
    g_p = 1.0 / jnp.sum(jnp.exp(gl - gmax), axis=1, keepdims=True)
    lo = N_GROUPS + EXPERTS_PER_GROUP * g_sel
    el = jnp.where((lane >= lo) & (lane < lo + EXPERTS_PER_GROUP), logits, ninf)
    v1 = jnp.max(el, axis=1, keepdims=True)
    i1 = jnp.min(jnp.where(el == v1, lane, LANES), axis=1, keepdims=True)
    el2 = jnp.where(lane == i1, ninf, el)
    v2 = jnp.max(el2, axis=1, keepdims=True)
    i2 = jnp.min(jnp.where(el2 == v2, lane, LANES), axis=1, keepdims=True)
    t = jnp.exp(v2 - v1)
    w1 = g_p / (1.0 + t)
    w2 = g_p * t / (1.0 + t)
    e1 = i1 - N_GROUPS
    e2 = i2 - N_GROUPS

    onehot = ((lane == e1) | (lane == e2)).astype(BF16)
    r_i = lax.broadcasted_iota(jnp.int32, (tm, tm), 0)
    c_i = lax.broadcasted_iota(jnp.int32, (tm, tm), 1)
    lower = (c_i < r_i).astype(BF16)
    before = jnp.dot(lower, onehot, preferred_element_type=F32) + carry_ref[...]
    rank1 = jnp.sum(jnp.where(lane == e1, before, 0.0), axis=1, keepdims=True)
    rank2 = jnp.sum(jnp.where(lane == e2, before, 0.0), axis=1, keepdims=True)
    carry_ref[...] = carry_ref[...] + jnp.sum(onehot.astype(F32), axis=0, keepdims=True)
    cnt_ref[...] = carry_ref[...]

    meta = jnp.where(lane == 0, e1.astype(F32),
           jnp.where(lane == 1, e2.astype(F32),
           jnp.where(lane == 2, w1,
           jnp.where(lane == 3, w2,
           jnp.where(lane == 4, rank1,
           jnp.where(lane == 5, rank2, 0.0))))))
    meta_ref[...] = meta


def _proj_norm_route(xres, a, w_bf16, pn_g, pn_b, ln_g, ln_b, wr_hi, wr_lo, br, *, pre_norm):
    tm = TM_ROUTE
    t, d = xres.shape
    row = lambda i: (i, 0)
    fix = lambda i: (0, 0)
    vec = lambda v: v.reshape(1, d)
    kern = functools.partial(_route_kernel, pre_norm=pre_norm)
    return pl.pallas_call(
        kern,
        out_shape=(jax.ShapeDtypeStruct((t, d), F32),
                   jax.ShapeDtypeStruct((t, LANES), F32),
                   jax.ShapeDtypeStruct((1, LANES), F32)),
        grid=(t // tm,),
        in_specs=[pl.BlockSpec((tm, d), row),
                  pl.BlockSpec((tm, d), row),
                  pl.BlockSpec((d, d), fix),
                  pl.BlockSpec((1, d), fix), pl.BlockSpec((1, d), fix),
                  pl.BlockSpec((1, d), fix), pl.BlockSpec((1, d), fix),
                  pl.BlockSpec((d, LANES), fix), pl.BlockSpec((d, LANES), fix),
                  pl.BlockSpec((1, LANES), fix)],
        out_specs=(pl.BlockSpec((tm, d), row),
                   pl.BlockSpec((tm, LANES), row),
                   pl.BlockSpec((1, LANES), fix)),
        scratch_shapes=[pltpu.VMEM((1, LANES), F32)],
        compiler_params=_cparams(("arbitrary",)),
        name="proj_norm_route",
    )(xres, a, w_bf16, vec(pn_g), vec(pn_b), vec(ln_g), vec(ln_b), wr_hi, wr_lo, br)


def _row_copy(src_ref, src_row, dst_ref, dst_row, sem):
    return pltpu.make_async_copy(src_ref.at[pl.ds(src_row, 1), :], dst_ref.at[pl.ds(dst_row, 1), :], sem)


def _scatter_kernel(pos_ref, x_ref, xs_in_ref, xs_ref, sem):
    del xs_in_ref
    tm = x_ref.shape[0]
    base = 2 * pl.program_id(0) * tm

    def start(tok, carry):
        _row_copy(x_ref, tok, xs_ref, pos_ref[base + 2 * tok], sem).start()
        _row_copy(x_ref, tok, xs_ref, pos_ref[base + 2 * tok + 1], sem).start()
        return carry

    def wait(tok, carry):
        _row_copy(x_ref, tok, xs_ref, pos_ref[base + 2 * tok], sem).wait()
        _row_copy(x_ref, tok, xs_ref, pos_ref[base + 2 * tok + 1], sem).wait()
        return carry

    lax.fori_loop(0, tm, start, 0, unroll=DMA_LOOP_UNROLL)
    lax.fori_loop(0, tm, wait, 0, unroll=DMA_LOOP_UNROLL)


def _scatter_rows(pos, x1):
    tm = TM_MOVE
    t, d = x1.shape
    xs0 = jnp.zeros((N_SORTED_ROWS, d), F32)
    return pl.pallas_call(
        _scatter_kernel,
        out_shape=jax.ShapeDtypeStruct((N_SORTED_ROWS, d), F32),
        grid_spec=pltpu.PrefetchScalarGridSpec(
            num_scalar_prefetch=1,
            grid=(t // tm,),
            in_specs=[pl.BlockSpec((tm, d), lambda i, *_: (i, 0)),
                      pl.BlockSpec(memory_space=pl.ANY)],
            out_specs=pl.BlockSpec(memory_space=pl.ANY),
            scratch_shapes=[pltpu.SemaphoreType.DMA]),
        input_output_aliases={2: 0},
        compiler_params=_cparams(("arbitrary",)),
        name="moe_scatter",
    )(pos, x1, xs0)


def _expert_kernel(te_ref, nu_ref, xs_ref, wg_ref, wu_ref, wd_ref, y_ref, wg_bf, wu_bf, wd_bf):
    i = pl.program_id(0)
    used = i < nu_ref[0]
    new_expert = (i == 0) | (te_ref[i] != te_ref[jnp.maximum(i - 1, 0)])

    @pl.when(used & new_expert)
    def _():
        wg_bf[...] = wg_ref[...].astype(BF16)
        wu_bf[...] = wu_ref[...].astype(BF16)
        wd_bf[...] = wd_ref[...].astype(BF16)

    @pl.when(used)
    def _():
        xb = xs_ref[...].astype(BF16)
        g = jnp.dot(xb, wg_bf[...], preferred_element_type=F32)
        u = jnp.dot(xb, wu_bf[...], preferred_element_type=F32)
        h = (jax.nn.silu(g) * u).astype(BF16)
        y_ref[...] = jnp.dot(h, wd_bf[...], preferred_element_type=F32)

    @pl.when(jnp.logical_not(used))
    def _():
        y_ref[...] = jnp.zeros_like(y_ref)


def _expert_mlp(tile_expert, n_used, xs, wg, wu, wd):
    tm = TM_EXPERT
    d, f = D_MODEL, D_EXPERT
    return pl.pallas_call(
        _expert_kernel,
        out_shape=jax.ShapeDtypeStruct((N_SORTED_ROWS, d), F32),
        grid_spec=pltpu.PrefetchScalarGridSpec(
            num_scalar_prefetch=2,
            grid=(N_EXPERT_TILES,),
            in_specs=[pl.BlockSpec((tm, d), lambda i, te, nu: (i, 0)),
                      pl.BlockSpec((None, d, f), lambda i, te, nu: (te[i], 0, 0)),
                      pl.BlockSpec((None, d, f), lambda i, te, nu: (te[i], 0, 0)),
                      pl.BlockSpec((None, f, d), lambda i, te, nu: (te[i], 0, 0))],
            out_specs=pl.BlockSpec((tm, d), lambda i, te, nu: (i, 0)),
            scratch_shapes=[pltpu.VMEM((d, f), BF16), pltpu.VMEM((d, f), BF16), pltpu.VMEM((f, d), BF16)]),
        compiler_params=_cparams(("arbitrary",)),
        name="moe_experts",
    )(tile_expert, n_used, xs, wg, wu, wd)


def _combine_kernel(pos_ref, x_ref, meta_ref, y_ref, lng_ref, lnb_ref, o_ref, ybuf_ref, sem):
    tm = x_ref.shape[0]
    base = 2 * pl.program_id(0) * tm

    def start(tok, carry):
        _row_copy(y_ref, pos_ref[base + 2 * tok], ybuf_ref.at[0], tok, sem).start()
        _row_copy(y_ref, pos_ref[base + 2 * tok + 1], ybuf_ref.at[1], tok, sem).start()
        return carry

    def wait(tok, carry):
        _row_copy(y_ref, pos_ref[base + 2 * tok], ybuf_ref.at[0], tok, sem).wait()
        _row_copy(y_ref, pos_ref[base + 2 * tok + 1], ybuf_ref.at[1], tok, sem).wait()
        return carry

    lax.fori_loop(0, tm, start, 0, unroll=DMA_LOOP_UNROLL)
    lax.fori_loop(0, tm, wait, 0, unroll=DMA_LOOP_UNROLL)
    meta = meta_ref[...]
    lane = lax.broadcasted_iota(jnp.int32, meta.shape, 1)
    w1 = jnp.sum(jnp.where(lane == 2, meta, 0.0), axis=1, keepdims=True)
    w2 = jnp.sum(jnp.where(lane == 3, meta, 0.0), axis=1, keepdims=True)
    y = w1 * ybuf_ref[0] + w2 * ybuf_ref[1]
    o_ref[...] = _layer_norm(DEEPNORM_ALPHA * x_ref[...] + y, lng_ref[...], lnb_ref[...])


def _combine_norm(pos, x1, meta, y, ln_g, ln_b):
    tm = TM_MOVE
    t, d = x1.shape
    return pl.pallas_call(
        _combine_kernel,
        out_shape=jax.ShapeDtypeStruct((t, d), F32),
        grid_spec=pltpu.PrefetchScalarGridSpec(
            num_scalar_prefetch=1,
            grid=(t // tm,),
            in_specs=[pl.BlockSpec((tm, d), lambda i, *_: (i, 0)),
                      pl.BlockSpec((tm, LANES), lambda i, *_: (i, 0)),
                      pl.BlockSpec(memory_space=pl.ANY),
                      pl.BlockSpec((1, d), lambda i, *_: (0, 0)),
                      pl.BlockSpec((1, d), lambda i, *_: (0, 0))],
            out_specs=pl.BlockSpec((tm, d), lambda i, *_: (i, 0)),
            scratch_shapes=[pltpu.VMEM((2, tm, d), F32), pltpu.SemaphoreType.DMA]),
        compiler_params=_cparams(("arbitrary",)),
        name="moe_combine_norm",
    )(pos, x1, meta, y, ln_g.reshape(1, d), ln_b.reshape(1, d))


def _moe_layer(layer, x1, meta, counts, wg, wu, wd, ln_g, ln_b):
    t = x1.shape[0]
    e12 = meta[:, 0:2].astype(jnp.int32).reshape(2 * t)
    r12 = meta[:, 4:6].astype(jnp.int32).reshape(2 * t)
    cnt = counts[0, :N_EXPERTS].astype(jnp.int32)
    padded = ((cnt + TM_EXPERT - 1) // TM_EXPERT) * TM_EXPERT
    ends = jnp.cumsum(padded)
    off = (ends - padded).astype(jnp.int32)
    pos = off[e12] + r12
    tile_start = jnp.arange(N_EXPERT_TILES, dtype=jnp.int32) * TM_EXPERT
    tile_expert = jnp.sum((tile_start[:, None] >= ends[None, :]).astype(jnp.int32), axis=1)
    tile_expert = jnp.minimum(tile_expert, N_EXPERTS - 1) + layer * N_EXPERTS
    n_used = (ends[-1:] // TM_EXPERT).astype(jnp.int32)
    xs = _scatter_rows(pos, x1)
    y = _expert_mlp(tile_expert, n_used, xs, wg, wu, wd)
    return _combine_norm(pos, x1, meta, y, ln_g, ln_b)


def _router_weights(w_grp, b_grp, w_rt, b_rt):
    d = w_grp.shape[0]
    pad = LANES - N_GROUPS - N_EXPERTS
    wr = jnp.concatenate([w_grp, w_rt.reshape(d, N_EXPERTS), jnp.zeros((d, pad), F32)], axis=1)
    br = jnp.concatenate([b_grp, b_rt.reshape(N_EXPERTS), jnp.zeros((pad,), F32)]).reshape(1, LANES)
    wr_hi = _round_to_bf16(wr)
    wr_lo = _round_to_bf16(wr - wr_hi)
    return wr_hi.astype(BF16), wr_lo.astype(BF16), br


def kernel(x, attn_w_qkv, attn_w_o, conv_w_pw1, conv_b_pw1, conv_w_dw, conv_b_dw, conv_ln_g, conv_ln_b,
           conv_w_pw2, mix_ln_g, mix_ln_b, moe_w_grp, moe_b_grp, moe_w_rt, moe_b_rt, moe_w_gate, moe_w_up,
           moe_w_down, ffn_ln_g, ffn_ln_b):
    b, s, d = x.shape
    assert (b, s, d) == (1, SEQ, D_MODEL)
    xt = x.reshape(s, d)
    ones = jnp.ones((d,), F32)
    zeros = jnp.zeros((d,), F32)

    n_all = DEPTH * N_EXPERTS
    moe_w = (moe_w_gate.reshape(n_all, D_MODEL, D_EXPERT), moe_w_up.reshape(n_all, D_MODEL, D_EXPERT),
             moe_w_down.reshape(n_all, D_EXPERT, D_MODEL))

    qkv = _qkv_proj(xt, attn_w_qkv[0].astype(BF16))
    o = _moba_attention(qkv, _moba_key_table(), _moba_slope_table())
    wr_hi, wr_lo, br = _router_weights(moe_w_grp[0], moe_b_grp[0], moe_w_rt[0], moe_b_rt[0])
    x1, meta, counts = _proj_norm_route(xt, o, attn_w_o[0].astype(BF16), ones, zeros, mix_ln_g[0], mix_ln_b[0],
                                        wr_hi, wr_lo, br, pre_norm=False)
    x2 = _moe_layer(0, x1, meta, counts, *moe_w, ffn_ln_g[0], ffn_ln_b[0])

    g = _pw1_glu(x2, conv_w_pw1[0].astype(BF16), conv_b_pw1[0])
    c = _dwconv(g, conv_w_dw[0], conv_b_dw[0])
    wr_hi, wr_lo, br = _router_weights(moe_w_grp[1], moe_b_grp[1], moe_w_rt[1], moe_b_rt[1])
    x3, meta, counts = _proj_norm_route(x2, c, conv_w_pw2[0].astype(BF16), conv_ln_g[0], conv_ln_b[0],
                                        mix_ln_g[1], mix_ln_b[1], wr_hi, wr_lo, br, pre_norm=True)
    x4 = _moe_layer(1, x3, meta, counts, *moe_w, ffn_ln_g[1], ffn_ln_b[1])
    return x4.reshape(b, s, d)
```

```python
import functools

import jax
import jax.numpy as jnp
from jax import lax
from jax.experimental import pallas as pl
from jax.experimental.pallas import tpu as pltpu

F32 = jnp.float32
BF16 = jnp.bfloat16

D_MODEL = 2048
SEQ = 16384
DEPTH = 2
N_HEADS = 16
HEAD_DIM = D_MODEL // N_HEADS
MOBA_BLOCK = 256
MOBA_TOPK = 3
N_BLOCKS = SEQ // MOBA_BLOCK
CONV_WIDTH = 31
N_GROUPS = 4
EXPERTS_PER_GROUP = 8
N_EXPERTS = N_GROUPS * EXPERTS_PER_GROUP
D_EXPERT = D_MODEL // 4
LN_EPS = 1e-5
NEG_INF = -1e30
DEEPNORM_ALPHA = (2.0 * DEPTH) ** 0.25

LANES = 128
SUBLANES = 8
CONV_HALO = 32
TM_ROUTE = 512
TM_MOVE = 256
DMA_LOOP_UNROLL = 8
TM_EXPERT = 256
N_EXPERT_TILES = (2 * SEQ) // TM_EXPERT + N_EXPERTS
N_SORTED_ROWS = N_EXPERT_TILES * TM_EXPERT
VMEM_LIMIT = 56 * 1024 * 1024


def _round_to_bf16(v):
    return lax.reduce_precision(v, exponent_bits=8, mantissa_bits=7)


def _cparams(sem):
    return pltpu.CompilerParams(dimension_semantics=sem, vmem_limit_bytes=VMEM_LIMIT)


HALF = D_MODEL // 2
_HI16 = 0xFFFF0000


def _pack_bf16_pair(lo, hi):
    lo_b = lax.bitcast_convert_type(lo.astype(BF16).astype(F32), jnp.uint32)
    hi_b = lax.bitcast_convert_type(hi.astype(BF16).astype(F32), jnp.uint32)
    return (hi_b & jnp.uint32(_HI16)) | (lo_b >> 16)


def _unpack_bf16_pair(u):
    lo = lax.bitcast_convert_type(u << 16, F32)
    hi = lax.bitcast_convert_type(u & jnp.uint32(_HI16), F32)
    return lo, hi


def _qkv_kernel(x_ref, w_ref, o_ref, *, n_scaled_tiles, scale):
    acc = jnp.dot(x_ref[...].astype(BF16), w_ref[...], preferred_element_type=F32)
    s = jnp.where(pl.program_id(1) < n_scaled_tiles, scale, 1.0).astype(F32)
    o_ref[...] = (acc * s).astype(o_ref.dtype)


def _qkv_proj(x, w_bf16):
    tm, tn = 1024, 1024
    t, d = x.shape
    n = w_bf16.shape[1]
    kern = functools.partial(_qkv_kernel, n_scaled_tiles=D_MODEL // tn, scale=HEAD_DIM ** -0.5)
    return pl.pallas_call(
        kern,
        out_shape=jax.ShapeDtypeStruct((t, n), BF16),
        grid=(t // tm, n // tn),
        in_specs=[pl.BlockSpec((tm, d), lambda i, j: (i, 0)),
                  pl.BlockSpec((d, tn), lambda i, j: (0, j))],
        out_specs=pl.BlockSpec((tm, tn), lambda i, j: (i, j)),
        compiler_params=_cparams(("parallel", "arbitrary")),
        name="qkv_proj",
    )(x, w_bf16)


def _glu_kernel(x_ref, wa_ref, wb_ref, ba_ref, bb_ref, o_ref):
    xb = x_ref[...].astype(BF16)
    a = jnp.dot(xb, wa_ref[...], preferred_element_type=F32) + ba_ref[...]
    b = jnp.dot(xb, wb_ref[...], preferred_element_type=F32) + bb_ref[...]
    o_ref[...] = a * jax.nn.sigmoid(b)


def _pw1_glu(x, w_bf16, bias):
    tm, tn = 1024, 512
    t, d = x.shape
    nb = d // tn
    bias2 = bias.reshape(1, 2 * d)
    return pl.pallas_call(
        _glu_kernel,
        out_shape=jax.ShapeDtypeStruct((t, d), F32),
        grid=(t // tm, nb),
        in_specs=[pl.BlockSpec((tm, d), lambda i, j: (i, 0)),
                  pl.BlockSpec((d, tn), lambda i, j: (0, j)),
                  pl.BlockSpec((d, tn), lambda i, j: (0, j + nb)),
                  pl.BlockSpec((1, tn), lambda i, j: (0, j)),
                  pl.BlockSpec((1, tn), lambda i, j: (0, j + nb))],
        out_specs=pl.BlockSpec((tm, tn), lambda i, j: (i, j)),
        compiler_params=_cparams(("parallel", "arbitrary")),
        name="pw1_glu",
    )(x, w_bf16, w_bf16, bias2, bias2)


N_SLOPE_SPLIT = 3
KPOS_LO = 128
HEADS_PER_STEP = 2
CHUNK_BLOCKS = 2
BIAS_LANE0 = N_BLOCKS
N_BIAS_LANES = 2 * N_SLOPE_SPLIT


def _moba_key_table():
    kpos = jnp.arange(SEQ, dtype=jnp.int32)
    onehot = (kpos[:, None] // MOBA_BLOCK == jnp.arange(N_BLOCKS)[None, :]).astype(BF16)
    hi = ((kpos // KPOS_LO) * KPOS_LO).astype(BF16)[:, None]
    lo = (kpos % KPOS_LO).astype(BF16)[:, None]
    pad = jnp.zeros((SEQ, LANES - N_BLOCKS - N_BIAS_LANES), BF16)
    return jnp.concatenate([onehot] + [hi] * N_SLOPE_SPLIT + [lo] * N_SLOPE_SPLIT + [pad], axis=1)


def _moba_slope_table():
    slopes = 2.0 ** (-8.0 * jnp.arange(1, N_HEADS + 1, dtype=F32) / N_HEADS)
    pieces = []
    rem = slopes
    for _ in range(N_SLOPE_SPLIT):
        p = _round_to_bf16(rem)
        pieces.append(p[:, None])
        rem = rem - p
    tab = jnp.concatenate([jnp.zeros((N_HEADS, BIAS_LANE0), F32)] + pieces + pieces
                          + [jnp.zeros((N_HEADS, LANES - BIAS_LANE0 - N_BIAS_LANES), F32)], axis=1)
    return tab.reshape(N_HEADS // HEADS_PER_STEP, HEADS_PER_STEP, LANES)


def _moba_kernel(q_ref, k_ref, vt_ref, c_ref, sl_ref, o_ref, kmean_ref, qa_ref, m_ref, l_ref, acc_ref,
                 s0_ref, s1_ref):
    blk = MOBA_BLOCK
    hp_n = HEADS_PER_STEP
    own = pl.program_id(1)
    nt = (((1,), (1,)), ((), ()))
    hcols = lambda hp: slice(hp * HEAD_DIM, (hp + 1) * HEAD_DIM)

    @pl.when(own == 0)
    def _():
        kmean_ref[...] = jnp.zeros_like(kmean_ref)

        def body(j, carry):
            kb = k_ref[pl.ds(pl.multiple_of(j * blk, blk), blk), :].astype(F32)
            mean = jnp.mean(kb, axis=0, keepdims=True)
            for hp in range(hp_n):
                kmean_ref[hp, pl.ds(j, 1), :] = mean[:, hcols(hp)]
            return carry

        lax.fori_loop(0, N_BLOCKS, body, 0)

    lane = lax.broadcasted_iota(jnp.int32, (blk, LANES), 1)
    minus_inf = jnp.float32(-jnp.inf)
    own_rows = pl.ds(pl.multiple_of(own * blk, blk), blk)
    key_i = lax.broadcasted_iota(jnp.int32, (blk, blk), 0)
    qry_i = lax.broadcasted_iota(jnp.int32, (blk, blk), 1)
    hrows = hcols
    for hp in range(hp_n):
        q = q_ref[:, hcols(hp)]
        km = kmean_ref[hp]
        km_hi = km.astype(BF16)
        km_lo = (km - km_hi.astype(F32)).astype(BF16)
        gate = (lax.dot_general(q, km_hi, nt, preferred_element_type=F32)
                + lax.dot_general(q, km_lo, nt, preferred_element_type=F32))
        gm = jnp.where(lane < own, gate, minus_inf)
        sel = lane < 0
        for r in range(MOBA_TOPK):
            mx = jnp.max(gm, axis=1, keepdims=True)
            idx = jnp.min(jnp.where(gm == mx, lane, LANES), axis=1, keepdims=True)
            hit = lane == idx
            sel = sel | (hit & (r < own))
            gm = jnp.where(hit, minus_inf, gm)
        slope_lanes = jnp.broadcast_to(sl_ref[pl.ds(hp, 1), :], (blk, LANES)).astype(BF16)
        aug = jnp.where(lane < N_BLOCKS, jnp.where(sel, 0.0, NEG_INF).astype(BF16), slope_lanes)
        qa_ref[hp] = jnp.concatenate([q, aug], axis=1)

        aug_own = jnp.where(lane < N_BLOCKS, jnp.zeros_like(slope_lanes), slope_lanes)
        k_aug = jnp.concatenate([k_ref[own_rows, hcols(hp)], c_ref[own_rows, :]], axis=1)
        s = lax.dot_general(k_aug, jnp.concatenate([q, aug_own], axis=1), nt, preferred_element_type=F32)
        s = jnp.where(key_i <= qry_i, s, NEG_INF)
        m0 = jnp.max(s, axis=0, keepdims=True)
        p = jnp.exp(s - m0)
        m_ref[hp] = m0
        l_ref[hp] = jnp.sum(p, axis=0, keepdims=True)
        acc_ref[hp] = jnp.dot(vt_ref[own, hrows(hp), :], p.astype(BF16), preferred_element_type=F32)

    width = CHUNK_BLOCKS * blk
    last_chunk = SEQ // width - 1

    def score_chunk(c, s_ref):
        rows = pl.ds(pl.multiple_of(c * width, width), width)
        c_rows = c_ref[rows, :]
        for hp in range(hp_n):
            k_aug = jnp.concatenate([k_ref[rows, hcols(hp)], c_rows], axis=1)
            s_ref[hp] = lax.dot_general(k_aug, qa_ref[hp], nt, preferred_element_type=F32)

    def absorb_chunk(c, s_ref):
        for hp in range(hp_n):
            s = s_ref[hp]
            m_old = m_ref[hp]
            m_new = jnp.maximum(m_old, jnp.max(s, axis=0, keepdims=True))
            alpha = jnp.exp(m_old - m_new)
            p = jnp.exp(s - m_new)
            l_ref[hp] = alpha * l_ref[hp] + jnp.sum(p, axis=0, keepdims=True)
            p = p.astype(BF16)
            pv = jnp.dot(vt_ref[c * CHUNK_BLOCKS, hrows(hp), :], p[:blk], preferred_element_type=F32)
            for b in range(1, CHUNK_BLOCKS):
                pv = pv + jnp.dot(vt_ref[c * CHUNK_BLOCKS + b, hrows(hp), :], p[b * blk:(b + 1) * blk],
                                  preferred_element_type=F32)
            acc_ref[hp] = alpha * acc_ref[hp] + pv
            m_ref[hp] = m_new

    score_chunk(0, s0_ref)

    def body(i, carry):
        c0 = 2 * i
        score_chunk(c0 + 1, s1_ref)
        absorb_chunk(c0, s0_ref)
        score_chunk(jnp.minimum(c0 + 2, last_chunk), s0_ref)
        absorb_chunk(c0 + 1, s1_ref)
        return carry

    n_pairs = lax.div(own + (2 * CHUNK_BLOCKS - 1), jnp.int32(2 * CHUNK_BLOCKS))
    lax.fori_loop(0, n_pairs, body, 0)
    for hp in range(hp_n):
        o_ref[:, hcols(hp)] = (acc_ref[hp] / l_ref[hp]).T.astype(o_ref.dtype)


def _moba_attention(qkv, key_table, slope_table):
    blk = MOBA_BLOCK
    hp_n = HEADS_PER_STEP
    w = hp_n * HEAD_DIM
    n_groups = N_HEADS // hp_n
    v_t = qkv[:, 2 * D_MODEL:].reshape(N_BLOCKS, blk, D_MODEL).transpose(0, 2, 1)
    return pl.pallas_call(
        _moba_kernel,
        out_shape=jax.ShapeDtypeStruct((SEQ, D_MODEL), BF16),
        grid=(n_groups, N_BLOCKS),
        in_specs=[pl.BlockSpec((blk, w), lambda h, i: (i, h)),
                  pl.BlockSpec((SEQ, w), lambda h, i: (0, n_groups + h)),
                  pl.BlockSpec((N_BLOCKS, w, blk), lambda h, i: (0, h, 0)),
                  pl.BlockSpec((SEQ, LANES), lambda h, i: (0, 0)),
                  pl.BlockSpec((None, hp_n, LANES), lambda h, i: (h, 0, 0))],
        out_specs=pl.BlockSpec((blk, w), lambda h, i: (i, h)),
        scratch_shapes=[pltpu.VMEM((hp_n, LANES, HEAD_DIM), F32),
                        pltpu.VMEM((hp_n, blk, 2 * LANES), BF16),
                        pltpu.VMEM((hp_n, 1, blk), F32),
                        pltpu.VMEM((hp_n, 1, blk), F32),
                        pltpu.VMEM((hp_n, HEAD_DIM, blk), F32),
                        pltpu.VMEM((hp_n, CHUNK_BLOCKS * blk, blk), F32),
                        pltpu.VMEM((hp_n, CHUNK_BLOCKS * blk, blk), F32)],
        compiler_params=_cparams(("arbitrary", "arbitrary")),
        name="moba_attention",
    )(qkv, qkv, v_t, key_table, slope_table)


def _dwconv_kernel(cur_ref, prev_ref, w_ref, b_ref, o_ref, win_ref, sh_ref, *, row_chunk):
    tm = cur_ref.shape[0]
    keep = (pl.program_id(0) > 0).astype(F32)
    win_ref[pl.ds(0, CONV_HALO), :] = prev_ref[...] * keep
    win_ref[pl.ds(CONV_HALO, tm), :] = cur_ref[...]
    first = CONV_HALO - (CONV_WIDTH - 1)
    n_sh = sh_ref.shape[1]
    for b in range(1, SUBLANES):
        sh_ref[b - 1] = win_ref[pl.ds(b, n_sh), :]
    for rc in range(tm // row_chunk):
        acc = jnp.broadcast_to(b_ref[...], (row_chunk, cur_ref.shape[1]))
        for k in range(CONV_WIDTH):
            b = (first + k) % SUBLANES
            a = first + k - b + rc * row_chunk
            src = win_ref if b == 0 else sh_ref.at[b - 1]
            acc = acc + w_ref[pl.ds(k, 1), :] * src[pl.ds(a, row_chunk), :]
        o_ref[pl.ds(rc * row_chunk, row_chunk), :] = acc


def _dwconv(g, w_dw, b_dw):
    tm, tc = 512, 512
    t, d = g.shape
    w_pad = jnp.concatenate([w_dw, jnp.zeros((CONV_HALO - CONV_WIDTH, d), F32)], axis=0)
    halo_per_tile = tm // CONV_HALO
    kern = functools.partial(_dwconv_kernel, row_chunk=64)
    return pl.pallas_call(
        kern,
        out_shape=jax.ShapeDtypeStruct((t, d), F32),
        grid=(t // tm, d // tc),
        in_specs=[pl.BlockSpec((tm, tc), lambda i, j: (i, j)),
                  pl.BlockSpec((CONV_HALO, tc), lambda i, j: (jnp.maximum(i * halo_per_tile - 1, 0), j)),
                  pl.BlockSpec((CONV_HALO, tc), lambda i, j: (0, j)),
                  pl.BlockSpec((1, tc), lambda i, j: (0, j))],
        out_specs=pl.BlockSpec((tm, tc), lambda i, j: (i, j)),
        scratch_shapes=[pltpu.VMEM((tm + CONV_HALO, tc), F32),
                        pltpu.VMEM((SUBLANES - 1, tm + CONV_HALO - SUBLANES, tc), F32)],
        compiler_params=_cparams(("parallel", "arbitrary")),
        name="dwconv",
    )(g, g, w_pad, b_dw.reshape(1, d))


def _layer_norm(z, g, b):
    mu = jnp.mean(z, axis=-1, keepdims=True)
    zc = z - mu
    var = jnp.mean(zc * zc, axis=-1, keepdims=True)
    return zc * lax.rsqrt(var + LN_EPS) * g + b


def _route_kernel(xres_ref, a_ref, w_ref, png_ref, pnb_ref, lng_ref, lnb_ref,
                  wr_hi_ref, wr_lo_ref, br_ref,
                  x1_ref, xp_ref, meta_ref, cnt_ref, carry_ref, *, pre_norm):
    tm = xres_ref.shape[0]

    @pl.when(pl.program_id(0) == 0)
    def _():
        carry_ref[...] = jnp.zeros_like(carry_ref)

    a = a_ref[...]
    if pre_norm:
        a = jax.nn.silu(_layer_norm(a.astype(F32), png_ref[...], pnb_ref[...]))
    y = jnp.dot(a.astype(BF16), w_ref[...], preferred_element_type=F32)
    x1 = _layer_norm(DEEPNORM_ALPHA * xres_ref[...] + y, lng_ref[...], lnb_ref[...])
    x1_ref[...] = x1
    xp_ref[...] = _pack_bf16_pair(x1[:, :HALF], x1[:, HALF:])

    x_hi = x1.astype(BF16)
    x_lo = (x1 - x_hi.astype(F32)).astype(BF16)
    logits = (jnp.dot(x_hi, wr_hi_ref[...], preferred_element_type=F32)
              + jnp.dot(x_lo, wr_hi_ref[...], preferred_element_type=F32)
              + jnp.dot(x_hi, wr_lo_ref[...], preferred_element_type=F32)
              + br_ref[...])
    lane = lax.broadcasted_iota(jnp.int32, logits.shape, 1)
    ninf = jnp.float32(-jnp.inf)
    gl = jnp.where(lane < N_GROUPS, logits, ninf)
    gmax = jnp.max(gl, axis=1, keepdims=True)
    g_sel = jnp.min(jnp.where(gl == gmax, lane, LANES), axis=1, keepdims=True)
    g_p = 1.0 / jnp.sum(jnp.exp(gl - gmax), axis=1, keepdims=True)
    lo = N_GROUPS + EXPERTS_PER_GROUP * g_sel
    el = jnp.where((lane >= lo) & (lane < lo + EXPERTS_PER_GROUP), logits, ninf)
    v1 = jnp.max(el, axis=1, keepdims=True)
    i1 = jnp.min(jnp.where(el == v1, lane, LANES), axis=1, keepdims=True)
    el2 = jnp.where(lane == i1, ninf, el)
    v2 = jnp.max(el2, axis=1, keepdims=True)
    i2 = jnp.min(jnp.where(el2 == v2, lane, LANES), axis=1, keepdims=True)
    t = jnp.exp(v2 - v1)
    w1 = g_p / (1.0 + t)
    w2 = g_p * t / (1.0 + t)
    e1 = i1 - N_GROUPS
    e2 = i2 - N_GROUPS

    onehot = ((lane == e1) | (lane == e2)).astype(BF16)
    r_i = lax.broadcasted_iota(jnp.int32, (tm, tm), 0)
    c_i = lax.broadcasted_iota(jnp.int32, (tm, tm), 1)
    lower = (c_i < r_i).astype(BF16)
    before = jnp.dot(lower, onehot, preferred_element_type=F32) + carry_ref[...]
    rank1 = jnp.sum(jnp.where(lane == e1, before, 0.0), axis=1, keepdims=True)
    rank2 = jnp.sum(jnp.where(lane == e2, before, 0.0), axis=1, keepdims=True)
    carry_ref[...] = carry_ref[...] + jnp.sum(onehot.astype(F32), axis=0, keepdims=True)
    cnt_ref[...] = carry_ref[...]

    meta = jnp.where(lane == 0, e1.astype(F32),
           jnp.where(lane == 1, e2.astype(F32),
           jnp.where(lane == 2, w1,
           jnp.where(lane == 3, w2,
           jnp.where(lane == 4, rank1,
           jnp.where(lane == 5, rank2, 0.0))))))
    meta_ref[...] = meta


def _proj_norm_route(xres, a, w_bf16, pn_g, pn_b, ln_g, ln_b, wr_hi, wr_lo, br, *, pre_norm):
    tm = TM_ROUTE
    t, d = xres.shape
    row = lambda i: (i, 0)
    fix = lambda i: (0, 0)
    vec = lambda v: v.reshape(1, d)
    kern = functools.partial(_route_kernel, pre_norm=pre_norm)
    return pl.pallas_call(
        kern,
        out_shape=(jax.ShapeDtypeStruct((t, d), F32),
                   jax.ShapeDtypeStruct((t, HALF), jnp.uint32),
                   jax.ShapeDtypeStruct((t, LANES), F32),
                   jax.ShapeDtypeStruct((1, LANES), F32)),
        grid=(t // tm,),
        in_specs=[pl.BlockSpec((tm, d), row),
                  pl.BlockSpec((tm, d), row),
                  pl.BlockSpec((d, d), fix),
                  pl.BlockSpec((1, d), fix), pl.BlockSpec((1, d), fix),
                  pl.BlockSpec((1, d), fix), pl.BlockSpec((1, d), fix),
                  pl.BlockSpec((d, LANES), fix), pl.BlockSpec((d, LANES), fix),
                  pl.BlockSpec((1, LANES), fix)],
        out_specs=(pl.BlockSpec((tm, d), row),
                   pl.BlockSpec((tm, HALF), row),
                   pl.BlockSpec((tm, LANES), row),
                   pl.BlockSpec((1, LANES), fix)),
        scratch_shapes=[pltpu.VMEM((1, LANES), F32)],
        compiler_params=_cparams(("arbitrary",)),
        name="proj_norm_route",
    )(xres, a, w_bf16, vec(pn_g), vec(pn_b), vec(ln_g), vec(ln_b), wr_hi, wr_lo, br)


def _row_copy(src_ref, src_row, dst_ref, dst_row, sem):
    return pltpu.make_async_copy(src_ref.at[pl.ds(src_row, 1), :], dst_ref.at[pl.ds(dst_row, 1), :], sem)


def _scatter_kernel(pos_ref, x_ref, xs_in_ref, xs_ref, sem):
    del xs_in_ref
    tm = x_ref.shape[0]
    base = 2 * pl.program_id(0) * tm

    def start(tok, carry):
        _row_copy(x_ref, tok, xs_ref, pos_ref[base + 2 * tok], sem).start()
        _row_copy(x_ref, tok, xs_ref, pos_ref[base + 2 * tok + 1], sem).start()
        return carry

    def wait(tok, carry):
        _row_copy(x_ref, tok, xs_ref, pos_ref[base + 2 * tok], sem).wait()
        _row_copy(x_ref, tok, xs_ref, pos_ref[base + 2 * tok + 1], sem).wait()
        return carry

    lax.fori_loop(0, tm, start, 0, unroll=DMA_LOOP_UNROLL)
    lax.fori_loop(0, tm, wait, 0, unroll=DMA_LOOP_UNROLL)


def _scatter_rows(pos, xp):
    tm = TM_MOVE
    t, d = xp.shape
    xs0 = jnp.zeros((N_SORTED_ROWS, d), xp.dtype)
    return pl.pallas_call(
        _scatter_kernel,
        out_shape=jax.ShapeDtypeStruct((N_SORTED_ROWS, d), xp.dtype),
        grid_spec=pltpu.PrefetchScalarGridSpec(
            num_scalar_prefetch=1,
            grid=(t // tm,),
            in_specs=[pl.BlockSpec((tm, d), lambda i, *_: (i, 0)),
                      pl.BlockSpec(memory_space=pl.ANY)],
            out_specs=pl.BlockSpec(memory_space=pl.ANY),
            scratch_shapes=[pltpu.SemaphoreType.DMA]),
        input_output_aliases={2: 0},
        compiler_params=_cparams(("arbitrary",)),
        name="moe_scatter",
    )(pos, xp, xs0)


def _expert_kernel(te_ref, nu_ref, xs_ref, wg_ref, wu_ref, wd_ref, y_ref, wg_bf, wu_bf, wd_bf):
    i = pl.program_id(0)
    used = i < nu_ref[0]
    new_expert = (i == 0) | (te_ref[i] != te_ref[jnp.maximum(i - 1, 0)])

    @pl.when(used & new_expert)
    def _():
        wg_bf[...] = wg_ref[...].astype(BF16)
        wu_bf[...] = wu_ref[...].astype(BF16)
        wd_bf[...] = wd_ref[...].astype(BF16)

    @pl.when(used)
    def _():
        x_lo, x_hi = _unpack_bf16_pair(xs_ref[...])
        x_lo = x_lo.astype(BF16)
        x_hi = x_hi.astype(BF16)
        lo_rows = pl.ds(0, HALF)
        hi_rows = pl.ds(HALF, HALF)
        g = (jnp.dot(x_lo, wg_bf[lo_rows, :], preferred_element_type=F32)
             + jnp.dot(x_hi, wg_bf[hi_rows, :], preferred_element_type=F32))
        u = (jnp.dot(x_lo, wu_bf[lo_rows, :], preferred_element_type=F32)
             + jnp.dot(x_hi, wu_bf[hi_rows, :], preferred_element_type=F32))
        h = (jax.nn.silu(g) * u).astype(BF16)
        y = jnp.dot(h, wd_bf[...], preferred_element_type=F32)
        y_ref[...] = _pack_bf16_pair(y[:, :HALF], y[:, HALF:])

    @pl.when(jnp.logical_not(used))
    def _():
        y_ref[...] = jnp.zeros_like(y_ref)


def _expert_mlp(tile_expert, n_used, xs, wg, wu, wd):
    tm = TM_EXPERT
    d, f = D_MODEL, D_EXPERT
    return pl.pallas_call(
        _expert_kernel,
        out_shape=jax.ShapeDtypeStruct((N_SORTED_ROWS, HALF), jnp.uint32),
        grid_spec=pltpu.PrefetchScalarGridSpec(
            num_scalar_prefetch=2,
            grid=(N_EXPERT_TILES,),
            in_specs=[pl.BlockSpec((tm, HALF), lambda i, te, nu: (i, 0)),
                      pl.BlockSpec((None, d, f), lambda i, te, nu: (te[i], 0, 0)),
                      pl.BlockSpec((None, d, f), lambda i, te, nu: (te[i], 0, 0)),
                      pl.BlockSpec((None, f, d), lambda i, te, nu: (te[i], 0, 0))],
            out_specs=pl.BlockSpec((tm, HALF), lambda i, te, nu: (i, 0)),
            scratch_shapes=[pltpu.VMEM((d, f), BF16), pltpu.VMEM((d, f), BF16), pltpu.VMEM((f, d), BF16)]),
        compiler_params=_cparams(("arbitrary",)),
        name="moe_experts",
    )(tile_expert, n_used, xs, wg, wu, wd)


def _combine_kernel(pos_ref, x_ref, meta_ref, y_ref, lng_ref, lnb_ref, o_ref, ybuf_ref, sem):
    tm = x_ref.shape[0]
    base = 2 * pl.program_id(0) * tm

    def start(tok, carry):
        _row_copy(y_ref, pos_ref[base + 2 * tok], ybuf_ref.at[0], tok, sem).start()
        _row_copy(y_ref, pos_ref[base + 2 * tok + 1], ybuf_ref.at[1], tok, sem).start()
        return carry

    def wait(tok, carry):
        _row_copy(y_ref, pos_ref[base + 2 * tok], ybuf_ref.at[0], tok, sem).wait()
        _row_copy(y_ref, pos_ref[base + 2 * tok + 1], ybuf_ref.at[1], tok, sem).wait()
        return carry

    lax.fori_loop(0, tm, start, 0, unroll=DMA_LOOP_UNROLL)
    lax.fori_loop(0, tm, wait, 0, unroll=DMA_LOOP_UNROLL)
    meta = meta_ref[...]
    lane = lax.broadcasted_iota(jnp.int32, meta.shape, 1)
    w1 = jnp.sum(jnp.where(lane == 2, meta, 0.0), axis=1, keepdims=True)
    w2 = jnp.sum(jnp.where(lane == 3, meta, 0.0), axis=1, keepdims=True)
    y1_lo, y1_hi = _unpack_bf16_pair(ybuf_ref[0])
    y2_lo, y2_hi = _unpack_bf16_pair(ybuf_ref[1])
    y = jnp.concatenate([w1 * y1_lo + w2 * y2_lo, w1 * y1_hi + w2 * y2_hi], axis=1)
    o_ref[...] = _layer_norm(DEEPNORM_ALPHA * x_ref[...] + y, lng_ref[...], lnb_ref[...])


def _combine_norm(pos, x1, meta, y, ln_g, ln_b):
    tm = TM_MOVE
    t, d = x1.shape
    return pl.pallas_call(
        _combine_kernel,
        out_shape=jax.ShapeDtypeStruct((t, d), F32),
        grid_spec=pltpu.PrefetchScalarGridSpec(
            num_scalar_prefetch=1,
            grid=(t // tm,),
            in_specs=[pl.BlockSpec((tm, d), lambda i, *_: (i, 0)),
                      pl.BlockSpec((tm, LANES), lambda i, *_: (i, 0)),
                      pl.BlockSpec(memory_space=pl.ANY),
                      pl.BlockSpec((1, d), lambda i, *_: (0, 0)),
                      pl.BlockSpec((1, d), lambda i, *_: (0, 0))],
            out_specs=pl.BlockSpec((tm, d), lambda i, *_: (i, 0)),
            scratch_shapes=[pltpu.VMEM((2, tm, HALF), jnp.uint32), pltpu.SemaphoreType.DMA]),
        compiler_params=_cparams(("arbitrary",)),
        name="moe_combine_norm",
    )(pos, x1, meta, y, ln_g.reshape(1, d), ln_b.reshape(1, d))


def _moe_layer(layer, x1, xp, meta, counts, wg, wu, wd, ln_g, ln_b):
    t = x1.shape[0]
    e12 = meta[:, 0:2].astype(jnp.int32).reshape(2 * t)
    r12 = meta[:, 4:6].astype(jnp.int32).reshape(2 * t)
    cnt = counts[0, :N_EXPERTS].astype(jnp.int32)
    padded = ((cnt + TM_EXPERT - 1) // TM_EXPERT) * TM_EXPERT
    ends = jnp.cumsum(padded)
    off = (ends - padded).astype(jnp.int32)
    pos = off[e12] + r12
    tile_start = jnp.arange(N_EXPERT_TILES, dtype=jnp.int32) * TM_EXPERT
    tile_expert = jnp.sum((tile_start[:, None] >= ends[None, :]).astype(jnp.int32), axis=1)
    tile_expert = jnp.minimum(tile_expert, N_EXPERTS - 1) + layer * N_EXPERTS
    n_used = (ends[-1:] // TM_EXPERT).astype(jnp.int32)
    xs = _scatter_rows(pos, xp)
    y = _expert_mlp(tile_expert, n_used, xs, wg, wu, wd)
    return _combine_norm(pos, x1, meta, y, ln_g, ln_b)


def _router_weights(w_grp, b_grp, w_rt, b_rt):
    d = w_grp.shape[0]
    pad = LANES - N_GROUPS - N_EXPERTS
    wr = jnp.concatenate([w_grp, w_rt.reshape(d, N_EXPERTS), jnp.zeros((d, pad), F32)], axis=1)
    br = jnp.concatenate([b_grp, b_rt.reshape(N_EXPERTS), jnp.zeros((pad,), F32)]).reshape(1, LANES)
    wr_hi = _round_to_bf16(wr)
    wr_lo = _round_to_bf16(wr - wr_hi)
    return wr_hi.astype(BF16), wr_lo.astype(BF16), br


def kernel(x, attn_w_qkv, attn_w_o, conv_w_pw1, conv_b_pw1, conv_w_dw, conv_b_dw, conv_ln_g, conv_ln_b,
           conv_w_pw2, mix_ln_g, mix_ln_b, moe_w_grp, moe_b_grp, moe_w_rt, moe_b_rt, moe_w_gate, moe_w_up,
           moe_w_down, ffn_ln_g, ffn_ln_b):
    b, s, d = x.shape
    assert (b, s, d) == (1, SEQ, D_MODEL)
    xt = x.reshape(s, d)
    ones = jnp.ones((d,), F32)
    zeros = jnp.zeros((d,), F32)

    n_all = DEPTH * N_EXPERTS
    moe_w = (moe_w_gate.reshape(n_all, D_MODEL, D_EXPERT), moe_w_up.reshape(n_all, D_MODEL, D_EXPERT),
             moe_w_down.reshape(n_all, D_EXPERT, D_MODEL))

    qkv = _qkv_proj(xt, attn_w_qkv[0].astype(BF16))
    o = _moba_attention(qkv, _moba_key_table(), _moba_slope_table())
    wr_hi, wr_lo, br = _router_weights(moe_w_grp[0], moe_b_grp[0], moe_w_rt[0], moe_b_rt[0])
    x1, xp, meta, counts = _proj_norm_route(xt, o, attn_w_o[0].astype(BF16), ones, zeros, mix_ln_g[0],
                                            mix_ln_b[0], wr_hi, wr_lo, br, pre_norm=False)
    x2 = _moe_layer(0, x1, xp, meta, counts, *moe_w, ffn_ln_g[0], ffn_ln_b[0])

    g = _pw1_glu(x2, conv_w_pw1[0].astype(BF16), conv_b_pw1[0])
    c = _dwconv(g, conv_w_dw[0], conv_b_dw[0])
    wr_hi, wr_lo, br = _router_weights(moe_w_grp[1], moe_b_grp[1], moe_w_rt[1], moe_b_rt[1])
    x3, xp, meta, counts = _proj_norm_route(x2, c, conv_w_pw2[0].astype(BF16), conv_ln_g[0], conv_ln_b[0],
                                            mix_ln_g[1], mix_ln_b[1], wr_hi, wr_lo, br, pre_norm=True)
    x4 = _moe_layer(1, x3, xp, meta, counts, *moe_w, ffn_ln_g[1], ffn_ln_b[1])
    return x4.reshape(b, s, d)
```

```python
import functools

import jax
import jax.numpy as jnp
from jax import lax
from jax.experimental import pallas as pl
from jax.experimental.pallas import tpu as pltpu

F32 = jnp.float32
BF16 = jnp.bfloat16

D_MODEL = 2048
SEQ = 16384
DEPTH = 2
N_HEADS = 16
HEAD_DIM = D_MODEL // N_HEADS
MOBA_BLOCK = 256
MOBA_TOPK = 3
N_BLOCKS = SEQ // MOBA_BLOCK
CONV_WIDTH = 31
N_GROUPS = 4
EXPERTS_PER_GROUP = 8
N_EXPERTS = N_GROUPS * EXPERTS_PER_GROUP
D_EXPERT = D_MODEL // 4
LN_EPS = 1e-5
NEG_INF = -1e30
DEEPNORM_ALPHA = (2.0 * DEPTH) ** 0.25

LANES = 128
SUBLANES = 8
CONV_HALO = 32
TM_ROUTE = 512
TM_MOVE = 256
DMA_LOOP_UNROLL = 8
TM_EXPERT = 256
N_EXPERT_TILES = (2 * SEQ) // TM_EXPERT + N_EXPERTS
N_SORTED_ROWS = N_EXPERT_TILES * TM_EXPERT
VMEM_LIMIT = 56 * 1024 * 1024


def _round_to_bf16(v):
    return lax.reduce_precision(v, exponent_bits=8, mantissa_bits=7)


def _cparams(sem):
    return pltpu.CompilerParams(dimension_semantics=sem, vmem_limit_bytes=VMEM_LIMIT)


HALF = D_MODEL // 2
_HI16 = 0xFFFF0000


def _pack_bf16_pair(lo, hi):
    lo_b = lax.bitcast_convert_type(lo.astype(BF16).astype(F32), jnp.uint32)
    hi_b = lax.bitcast_convert_type(hi.astype(BF16).astype(F32), jnp.uint32)
    return (hi_b & jnp.uint32(_HI16)) | (lo_b >> 16)


def _unpack_bf16_pair(u):
    lo = lax.bitcast_convert_type(u << 16, F32)
    hi = lax.bitcast_convert_type(u & jnp.uint32(_HI16), F32)
    return lo, hi


def _qkv_kernel(x_ref, w_ref, o_ref, *, n_scaled_tiles, scale):
    acc = jnp.dot(x_ref[...].astype(BF16), w_ref[...], preferred_element_type=F32)
    s = jnp.where(pl.program_id(1) < n_scaled_tiles, scale, 1.0).astype(F32)
    o_ref[...] = (acc * s).astype(o_ref.dtype)


def _qkv_proj(x, w_bf16):
    tm, tn = 1024, 1024
    t, d = x.shape
    n = w_bf16.shape[1]
    kern = functools.partial(_qkv_kernel, n_scaled_tiles=D_MODEL // tn, scale=HEAD_DIM ** -0.5)
    return pl.pallas_call(
        kern,
        out_shape=jax.ShapeDtypeStruct((t, n), BF16),
        grid=(t // tm, n // tn),
        in_specs=[pl.BlockSpec((tm, d), lambda i, j: (i, 0)),
                  pl.BlockSpec((d, tn), lambda i, j: (0, j))],
        out_specs=pl.BlockSpec((tm, tn), lambda i, j: (i, j)),
        compiler_params=_cparams(("parallel", "arbitrary")),
        name="qkv_proj",
    )(x, w_bf16)


def _glu_kernel(x_ref, wa_ref, wb_ref, ba_ref, bb_ref, o_ref):
    xb = x_ref[...].astype(BF16)
    a = jnp.dot(xb, wa_ref[...], preferred_element_type=F32) + ba_ref[...]
    b = jnp.dot(xb, wb_ref[...], preferred_element_type=F32) + bb_ref[...]
    o_ref[...] = a * jax.nn.sigmoid(b)


def _pw1_glu(x, w_bf16, bias):
    tm, tn = 1024, 512
    t, d = x.shape
    nb = d // tn
    bias2 = bias.reshape(1, 2 * d)
    return pl.pallas_call(
        _glu_kernel,
        out_shape=jax.ShapeDtypeStruct((t, d), F32),
        grid=(t // tm, nb),
        in_specs=[pl.BlockSpec((tm, d), lambda i, j: (i, 0)),
                  pl.BlockSpec((d, tn), lambda i, j: (0, j)),
                  pl.BlockSpec((d, tn), lambda i, j: (0, j + nb)),
                  pl.BlockSpec((1, tn), lambda i, j: (0, j)),
                  pl.BlockSpec((1, tn), lambda i, j: (0, j + nb))],
        out_specs=pl.BlockSpec((tm, tn), lambda i, j: (i, j)),
        compiler_params=_cparams(("parallel", "arbitrary")),
        name="pw1_glu",
    )(x, w_bf16, w_bf16, bias2, bias2)


N_SLOPE_SPLIT = 3
KPOS_LO = 128
HEADS_PER_STEP = 2
CHUNK_BLOCKS = 2
BIAS_LANE0 = N_BLOCKS
N_BIAS_LANES = 2 * N_SLOPE_SPLIT


def _moba_key_table():
    kpos = jnp.arange(SEQ, dtype=jnp.int32)
    onehot = (kpos[:, None] // MOBA_BLOCK == jnp.arange(N_BLOCKS)[None, :]).astype(BF16)
    hi = ((kpos // KPOS_LO) * KPOS_LO).astype(BF16)[:, None]
    lo = (kpos % KPOS_LO).astype(BF16)[:, None]
    pad = jnp.zeros((SEQ, LANES - N_BLOCKS - N_BIAS_LANES), BF16)
    return jnp.concatenate([onehot] + [hi] * N_SLOPE_SPLIT + [lo] * N_SLOPE_SPLIT + [pad], axis=1)


def _moba_slope_table():
    slopes = 2.0 ** (-8.0 * jnp.arange(1, N_HEADS + 1, dtype=F32) / N_HEADS)
    pieces = []
    rem = slopes
    for _ in range(N_SLOPE_SPLIT):
        p = _round_to_bf16(rem)
        pieces.append(p[:, None])
        rem = rem - p
    tab = jnp.concatenate([jnp.zeros((N_HEADS, BIAS_LANE0), F32)] + pieces + pieces
                          + [jnp.zeros((N_HEADS, LANES - BIAS_LANE0 - N_BIAS_LANES), F32)], axis=1)
    tab = jnp.broadcast_to(tab[:, :, None], (N_HEADS, LANES, MOBA_BLOCK))
    return tab.reshape(N_HEADS // HEADS_PER_STEP, HEADS_PER_STEP * LANES, MOBA_BLOCK)


def _moba_kernel(qt_ref, k_ref, vt_ref, c_ref, sl_ref, o_ref, kmean_ref, qa_ref, m_ref, l_ref, acc_ref,
                 s0_ref, s1_ref):
    blk = MOBA_BLOCK
    hp_n = HEADS_PER_STEP
    own = pl.program_id(1)
    hcols = lambda hp: slice(hp * HEAD_DIM, (hp + 1) * HEAD_DIM)

    @pl.when(own == 0)
    def _():
        kmean_ref[...] = jnp.zeros_like(kmean_ref)

        def body(j, carry):
            kb = k_ref[pl.ds(pl.multiple_of(j * blk, blk), blk), :].astype(F32)
            mean = jnp.mean(kb, axis=0, keepdims=True)
            for hp in range(hp_n):
                kmean_ref[hp, pl.ds(j, 1), :] = mean[:, hcols(hp)]
            return carry

        lax.fori_loop(0, N_BLOCKS, body, 0)

    row = lax.broadcasted_iota(jnp.int32, (LANES, blk), 0)
    minus_inf = jnp.float32(-jnp.inf)
    own_rows = pl.ds(pl.multiple_of(own * blk, blk), blk)
    key_i = lax.broadcasted_iota(jnp.int32, (blk, blk), 0)
    qry_i = lax.broadcasted_iota(jnp.int32, (blk, blk), 1)
    hrows = hcols
    for hp in range(hp_n):
        qt = qt_ref[hrows(hp), :]
        km = kmean_ref[hp]
        km_hi = km.astype(BF16)
        km_lo = (km - km_hi.astype(F32)).astype(BF16)
        gate = (jnp.dot(km_hi, qt, preferred_element_type=F32)
                + jnp.dot(km_lo, qt, preferred_element_type=F32))
        gm = jnp.where(row < own, gate, minus_inf)
        sel = row < 0
        for r in range(MOBA_TOPK):
            mx = jnp.max(gm, axis=0, keepdims=True)
            idx = jnp.min(jnp.where(gm == mx, row, LANES), axis=0, keepdims=True)
            hit = row == idx
            sel = sel | (hit & (r < own))
            gm = jnp.where(hit, minus_inf, gm)
        slope_rows = sl_ref[pl.ds(hp * LANES, LANES), :]
        aug = jnp.where(row < N_BLOCKS, jnp.where(sel, 0.0, NEG_INF), slope_rows).astype(BF16)
        qa_ref[hp] = jnp.concatenate([qt, aug], axis=0)

        aug_own = jnp.where(row < N_BLOCKS, 0.0, slope_rows).astype(BF16)
        k_aug = jnp.concatenate([k_ref[own_rows, hcols(hp)], c_ref[own_rows, :]], axis=1)
        s = jnp.dot(k_aug, jnp.concatenate([qt, aug_own], axis=0), preferred_element_type=F32)
        s = jnp.where(key_i <= qry_i, s, NEG_INF)
        m0 = jnp.max(s, axis=0, keepdims=True)
        p = jnp.exp(s - m0)
        m_ref[hp] = m0
        l_ref[hp] = jnp.sum(p, axis=0, keepdims=True)
        acc_ref[hp] = jnp.dot(vt_ref[own, hrows(hp), :], p.astype(BF16), preferred_element_type=F32)

    width = CHUNK_BLOCKS * blk
    last_chunk = SEQ // width - 1

    def score_chunk(c, s_ref):
        rows = pl.ds(pl.multiple_of(c * width, width), width)
        c_rows = c_ref[rows, :]
        for hp in range(hp_n):
            k_aug = jnp.concatenate([k_ref[rows, hcols(hp)], c_rows], axis=1)
            s_ref[hp] = jnp.dot(k_aug, qa_ref[hp], preferred_element_type=F32)

    def absorb_chunk(c, s_ref):
        for hp in range(hp_n):
            s = s_ref[hp]
            m_old = m_ref[hp]
            m_new = jnp.maximum(m_old, jnp.max(s, axis=0, keepdims=True))
            alpha = jnp.exp(m_old - m_new)
            p = jnp.exp(s - m_new)
            l_ref[hp] = alpha * l_ref[hp] + jnp.sum(p, axis=0, keepdims=True)
            p = p.astype(BF16)
            pv = jnp.dot(vt_ref[c * CHUNK_BLOCKS, hrows(hp), :], p[:blk], preferred_element_type=F32)
            for b in range(1, CHUNK_BLOCKS):
                pv = pv + jnp.dot(vt_ref[c * CHUNK_BLOCKS + b, hrows(hp), :], p[b * blk:(b + 1) * blk],
                                  preferred_element_type=F32)
            acc_ref[hp] = alpha * acc_ref[hp] + pv
            m_ref[hp] = m_new

    score_chunk(0, s0_ref)

    def body(i, carry):
        c0 = 2 * i
        score_chunk(c0 + 1, s1_ref)
        absorb_chunk(c0, s0_ref)
        score_chunk(jnp.minimum(c0 + 2, last_chunk), s0_ref)
        absorb_chunk(c0 + 1, s1_ref)
        return carry

    n_pairs = lax.div(own + (2 * CHUNK_BLOCKS - 1), jnp.int32(2 * CHUNK_BLOCKS))
    lax.fori_loop(0, n_pairs, body, 0)
    for hp in range(hp_n):
        o_ref[:, hcols(hp)] = (acc_ref[hp] / l_ref[hp]).T.astype(o_ref.dtype)


def _moba_attention(qkv, key_table, slope_table):
    blk = MOBA_BLOCK
    hp_n = HEADS_PER_STEP
    w = hp_n * HEAD_DIM
    n_groups = N_HEADS // hp_n
    q_t = qkv[:, :D_MODEL].reshape(N_BLOCKS, blk, D_MODEL).transpose(0, 2, 1)
    v_t = qkv[:, 2 * D_MODEL:].reshape(N_BLOCKS, blk, D_MODEL).transpose(0, 2, 1)
    return pl.pallas_call(
        _moba_kernel,
        out_shape=jax.ShapeDtypeStruct((SEQ, D_MODEL), BF16),
        grid=(n_groups, N_BLOCKS),
        in_specs=[pl.BlockSpec((None, w, blk), lambda h, i: (i, h, 0)),
                  pl.BlockSpec((SEQ, w), lambda h, i: (0, n_groups + h)),
                  pl.BlockSpec((N_BLOCKS, w, blk), lambda h, i: (0, h, 0)),
                  pl.BlockSpec((SEQ, LANES), lambda h, i: (0, 0)),
                  pl.BlockSpec((None, hp_n * LANES, blk), lambda h, i: (h, 0, 0))],
        out_specs=pl.BlockSpec((blk, w), lambda h, i: (i, h)),
        scratch_shapes=[pltpu.VMEM((hp_n, LANES, HEAD_DIM), F32),
                        pltpu.VMEM((hp_n, HEAD_DIM + LANES, blk), BF16),
                        pltpu.VMEM((hp_n, 1, blk), F32),
                        pltpu.VMEM((hp_n, 1, blk), F32),
                        pltpu.VMEM((hp_n, HEAD_DIM, blk), F32),
                        pltpu.VMEM((hp_n, CHUNK_BLOCKS * blk, blk), F32),
                        pltpu.VMEM((hp_n, CHUNK_BLOCKS * blk, blk), F32)],
        compiler_params=_cparams(("arbitrary", "arbitrary")),
        name="moba_attention",
    )(q_t, qkv, v_t, key_table, slope_table)


def _dwconv_kernel(cur_ref, prev_ref, w_ref, b_ref, o_ref, win_ref, sh_ref, *, row_chunk):
    tm = cur_ref.shape[0]
    keep = (pl.program_id(0) > 0).astype(F32)
    win_ref[pl.ds(0, CONV_HALO), :] = prev_ref[...] * keep
    win_ref[pl.ds(CONV_HALO, tm), :] = cur_ref[...]
    first = CONV_HALO - (CONV_WIDTH - 1)
    n_sh = sh_ref.shape[1]
    for b in range(1, SUBLANES):
        sh_ref[b - 1] = win_ref[pl.ds(b, n_sh), :]
    for rc in range(tm // row_chunk):
        acc = jnp.broadcast_to(b_ref[...], (row_chunk, cur_ref.shape[1]))
        for k in range(CONV_WIDTH):
            b = (first + k) % SUBLANES
            a = first + k - b + rc * row_chunk
            src = win_ref if b == 0 else sh_ref.at[b - 1]
            acc = acc + w_ref[pl.ds(k, 1), :] * src[pl.ds(a, row_chunk), :]
        o_ref[pl.ds(rc * row_chunk, row_chunk), :] = acc


def _dwconv(g, w_dw, b_dw):
    tm, tc = 512, 512
    t, d = g.shape
    w_pad = jnp.concatenate([w_dw, jnp.zeros((CONV_HALO - CONV_WIDTH, d), F32)], axis=0)
    halo_per_tile = tm // CONV_HALO
    kern = functools.partial(_dwconv_kernel, row_chunk=64)
    return pl.pallas_call(
        kern,
        out_shape=jax.ShapeDtypeStruct((t, d), F32),
        grid=(t // tm, d // tc),
        in_specs=[pl.BlockSpec((tm, tc), lambda i, j: (i, j)),
                  pl.BlockSpec((CONV_HALO, tc), lambda i, j: (jnp.maximum(i * halo_per_tile - 1, 0), j)),
                  pl.BlockSpec((CONV_HALO, tc), lambda i, j: (0, j)),
                  pl.BlockSpec((1, tc), lambda i, j: (0, j))],
        out_specs=pl.BlockSpec((tm, tc), lambda i, j: (i, j)),
        scratch_shapes=[pltpu.VMEM((tm + CONV_HALO, tc), F32),
                        pltpu.VMEM((SUBLANES - 1, tm + CONV_HALO - SUBLANES, tc), F32)],
        compiler_params=_cparams(("parallel", "arbitrary")),
        name="dwconv",
    )(g, g, w_pad, b_dw.reshape(1, d))


def _layer_norm(z, g, b):
    mu = jnp.mean(z, axis=-1, keepdims=True)
    zc = z - mu
    var = jnp.mean(zc * zc, axis=-1, keepdims=True)
    return zc * lax.rsqrt(var + LN_EPS) * g + b


def _route_kernel(xres_ref, a_ref, w_ref, png_ref, pnb_ref, lng_ref, lnb_ref,
                  wr_hi_ref, wr_lo_ref, br_ref,
                  x1_ref, xp_ref, meta_ref, cnt_ref, carry_ref, *, pre_norm):
    tm = xres_ref.shape[0]

    @pl.when(pl.program_id(0) == 0)
    def _():
        carry_ref[...] = jnp.zeros_like(carry_ref)

    a = a_ref[...]
    if pre_norm:
        a = jax.nn.silu(_layer_norm(a.astype(F32), png_ref[...], pnb_ref[...]))
    y = jnp.dot(a.astype(BF16), w_ref[...], preferred_element_type=F32)
    x1 = _layer_norm(DEEPNORM_ALPHA * xres_ref[...] + y, lng_ref[...], lnb_ref[...])
    x1_ref[...] = x1
    xp_ref[...] = _pack_bf16_pair(x1[:, :HALF], x1[:, HALF:])

    x_hi = x1.astype(BF16)
    x_lo = (x1 - x_hi.astype(F32)).astype(BF16)
    logits = (jnp.dot(x_hi, wr_hi_ref[...], preferred_element_type=F32)
              + jnp.dot(x_lo, wr_hi_ref[...], preferred_element_type=F32)
              + jnp.dot(x_hi, wr_lo_ref[...], preferred_element_type=F32)
              + br_ref[...])
    lane = lax.broadcasted_iota(jnp.int32, logits.shape, 1)
    ninf = jnp.float32(-jnp.inf)
    gl = jnp.where(lane < N_GROUPS, logits, ninf)
    gmax = jnp.max(gl, axis=1, keepdims=True)
    g_sel = jnp.min(jnp.where(gl == gmax, lane, LANES), axis=1, keepdims=True)
    g_p = 1.0 / jnp.sum(jnp.exp(gl - gmax), axis=1, keepdims=True)
    lo = N_GROUPS + EXPERTS_PER_GROUP * g_sel
    el = jnp.where((lane >= lo) & (lane < lo + EXPERTS_PER_GROUP), logits, ninf)
    v1 = jnp.max(el, axis=1, keepdims=True)
    i1 = jnp.min(jnp.where(el == v1, lane, LANES), axis=1, keepdims=True)
    el2 = jnp.where(lane == i1, ninf, el)
    v2 = jnp.max(el2, axis=1, keepdims=True)
    i2 = jnp.min(jnp.where(el2 == v2, lane, LANES), axis=1, keepdims=True)
    t = jnp.exp(v2 - v1)
    w1 = g_p / (1.0 + t)
    w2 = g_p * t / (1.0 + t)
    e1 = i1 - N_GROUPS
    e2 = i2 - N_GROUPS

    onehot = ((lane == e1) | (lane == e2)).astype(BF16)
    r_i = lax.broadcasted_iota(jnp.int32, (tm, tm), 0)
    c_i = lax.broadcasted_iota(jnp.int32, (tm, tm), 1)
    lower = (c_i < r_i).astype(BF16)
    before = jnp.dot(lower, onehot, preferred_element_type=F32) + carry_ref[...]
    rank1 = jnp.sum(jnp.where(lane == e1, before, 0.0), axis=1, keepdims=True)
    rank2 = jnp.sum(jnp.where(lane == e2, before, 0.0), axis=1, keepdims=True)
    carry_ref[...] = carry_ref[...] + jnp.sum(onehot.astype(F32), axis=0, keepdims=True)
    cnt_ref[...] = carry_ref[...]

    meta = jnp.where(lane == 0, e1.astype(F32),
           jnp.where(lane == 1, e2.astype(F32),
           jnp.where(lane == 2, w1,
           jnp.where(lane == 3, w2,
           jnp.where(lane == 4, rank1,
           jnp.where(lane == 5, rank2, 0.0))))))
    meta_ref[...] = meta


def _proj_norm_route(xres, a, w_bf16, pn_g, pn_b, ln_g, ln_b, wr_hi, wr_lo, br, *, pre_norm):
    tm = TM_ROUTE
    t, d = xres.shape
    row = lambda i: (i, 0)
    fix = lambda i: (0, 0)
    vec = lambda v: v.reshape(1, d)
    kern = functools.partial(_route_kernel, pre_norm=pre_norm)
    return pl.pallas_call(
        kern,
        out_shape=(jax.ShapeDtypeStruct((t, d), F32),
                   jax.ShapeDtypeStruct((t, HALF), jnp.uint32),
                   jax.ShapeDtypeStruct((t, LANES), F32),
                   jax.ShapeDtypeStruct((1, LANES), F32)),
        grid=(t // tm,),
        in_specs=[pl.BlockSpec((tm, d), row),
                  pl.BlockSpec((tm, d), row),
                  pl.BlockSpec((d, d), fix),
                  pl.BlockSpec((1, d), fix), pl.BlockSpec((1, d), fix),
                  pl.BlockSpec((1, d), fix), pl.BlockSpec((1, d), fix),
                  pl.BlockSpec((d, LANES), fix), pl.BlockSpec((d, LANES), fix),
                  pl.BlockSpec((1, LANES), fix)],
        out_specs=(pl.BlockSpec((tm, d), row),
                   pl.BlockSpec((tm, HALF), row),
                   pl.BlockSpec((tm, LANES), row),
                   pl.BlockSpec((1, LANES), fix)),
        scratch_shapes=[pltpu.VMEM((1, LANES), F32)],
        compiler_params=_cparams(("arbitrary",)),
        name="proj_norm_route",
    )(xres, a, w_bf16, vec(pn_g), vec(pn_b), vec(ln_g), vec(ln_b), wr_hi, wr_lo, br)


def _row_copy(src_ref, src_row, dst_ref, dst_row, sem):
    return pltpu.make_async_copy(src_ref.at[pl.ds(src_row, 1), :], dst_ref.at[pl.ds(dst_row, 1), :], sem)


def _scatter_kernel(pos_ref, x_ref, xs_in_ref, xs_ref, sem):
    del xs_in_ref
    tm = x_ref.shape[0]
    base = 2 * pl.program_id(0) * tm

    def start(tok, carry):
        _row_copy(x_ref, tok, xs_ref, pos_ref[base + 2 * tok], sem).start()
        _row_copy(x_ref, tok, xs_ref, pos_ref[base + 2 * tok + 1], sem).start()
        return carry

    def wait(tok, carry):
        _row_copy(x_ref, tok, xs_ref, pos_ref[base + 2 * tok], sem).wait()
        _row_copy(x_ref, tok, xs_ref, pos_ref[base + 2 * tok + 1], sem).wait()
        return carry

    lax.fori_loop(0, tm, start, 0, unroll=DMA_LOOP_UNROLL)
    lax.fori_loop(0, tm, wait, 0, unroll=DMA_LOOP_UNROLL)


def _scatter_rows(pos, xp):
    tm = TM_MOVE
    t, d = xp.shape
    xs0 = jnp.zeros((N_SORTED_ROWS, d), xp.dtype)
    return pl.pallas_call(
        _scatter_kernel,
        out_shape=jax.ShapeDtypeStruct((N_SORTED_ROWS, d), xp.dtype),
        grid_spec=pltpu.PrefetchScalarGridSpec(
            num_scalar_prefetch=1,
            grid=(t // tm,),
            in_specs=[pl.BlockSpec((tm, d), lambda i, *_: (i, 0)),
                      pl.BlockSpec(memory_space=pl.ANY)],
            out_specs=pl.BlockSpec(memory_space=pl.ANY),
            scratch_shapes=[pltpu.SemaphoreType.DMA]),
        input_output_aliases={2: 0},
        compiler_params=_cparams(("arbitrary",)),
        name="moe_scatter",
    )(pos, xp, xs0)


def _expert_kernel(te_ref, nu_ref, xs_ref, wg_ref, wu_ref, wd_ref, y_ref, wg_bf, wu_bf, wd_bf):
    i = pl.program_id(0)
    used = i < nu_ref[0]
    new_expert = (i == 0) | (te_ref[i] != te_ref[jnp.maximum(i - 1, 0)])

    @pl.when(used & new_expert)
    def _():
        wg_bf[...] = wg_ref[...].astype(BF16)
        wu_bf[...] = wu_ref[...].astype(BF16)
        wd_bf[...] = wd_ref[...].astype(BF16)

    @pl.when(used)
    def _():
        x_lo, x_hi = _unpack_bf16_pair(xs_ref[...])
        x_lo = x_lo.astype(BF16)
        x_hi = x_hi.astype(BF16)
        lo_rows = pl.ds(0, HALF)
        hi_rows = pl.ds(HALF, HALF)
        g = (jnp.dot(x_lo, wg_bf[lo_rows, :], preferred_element_type=F32)
             + jnp.dot(x_hi, wg_bf[hi_rows, :], preferred_element_type=F32))
        u = (jnp.dot(x_lo, wu_bf[lo_rows, :], preferred_element_type=F32)
             + jnp.dot(x_hi, wu_bf[hi_rows, :], preferred_element_type=F32))
        h = (jax.nn.silu(g) * u).astype(BF16)
        y = jnp.dot(h, wd_bf[...], preferred_element_type=F32)
        y_ref[...] = _pack_bf16_pair(y[:, :HALF], y[:, HALF:])

    @pl.when(jnp.logical_not(used))
    def _():
        y_ref[...] = jnp.zeros_like(y_ref)


def _expert_mlp(tile_expert, n_used, xs, wg, wu, wd):
    tm = TM_EXPERT
    d, f = D_MODEL, D_EXPERT
    return pl.pallas_call(
        _expert_kernel,
        out_shape=jax.ShapeDtypeStruct((N_SORTED_ROWS, HALF), jnp.uint32),
        grid_spec=pltpu.PrefetchScalarGridSpec(
            num_scalar_prefetch=2,
            grid=(N_EXPERT_TILES,),
            in_specs=[pl.BlockSpec((tm, HALF), lambda i, te, nu: (i, 0)),
                      pl.BlockSpec((None, d, f), lambda i, te, nu: (te[i], 0, 0)),
                      pl.BlockSpec((None, d, f), lambda i, te, nu: (te[i], 0, 0)),
                      pl.BlockSpec((None, f, d), lambda i, te, nu: (te[i], 0, 0))],
            out_specs=pl.BlockSpec((tm, HALF), lambda i, te, nu: (i, 0)),
            scratch_shapes=[pltpu.VMEM((d, f), BF16), pltpu.VMEM((d, f), BF16), pltpu.VMEM((f, d), BF16)]),
        compiler_params=_cparams(("arbitrary",)),
        name="moe_experts",
    )(tile_expert, n_used, xs, wg, wu, wd)


def _combine_kernel(pos_ref, x_ref, meta_ref, y_ref, lng_ref, lnb_ref, o_ref, ybuf_ref, sems):
    tm = x_ref.shape[0]
    i = pl.program_id(0)
    n_steps = pl.num_programs(0)

    def row_copies(step, slot, tok):
        base = 2 * step * tm
        return (_row_copy(y_ref, pos_ref[base + 2 * tok], ybuf_ref.at[slot, 0], tok, sems.at[slot]),
                _row_copy(y_ref, pos_ref[base + 2 * tok + 1], ybuf_ref.at[slot, 1], tok, sems.at[slot]))

    def request(step, slot):
        def body(tok, carry):
            for cp in row_copies(step, slot, tok):
                cp.start()
            return carry
        lax.fori_loop(0, tm, body, 0, unroll=DMA_LOOP_UNROLL)

    def wait(step, slot):
        def body(tok, carry):
            for cp in row_copies(step, slot, tok):
                cp.wait()
            return carry
        lax.fori_loop(0, tm, body, 0, unroll=DMA_LOOP_UNROLL)

    def finish(slot):
        @pl.when(i + 1 < n_steps)
        def _():
            request(i + 1, 1 - slot)

        wait(i, slot)
        meta = meta_ref[...]
        lane = lax.broadcasted_iota(jnp.int32, meta.shape, 1)
        w1 = jnp.sum(jnp.where(lane == 2, meta, 0.0), axis=1, keepdims=True)
        w2 = jnp.sum(jnp.where(lane == 3, meta, 0.0), axis=1, keepdims=True)
        y1_lo, y1_hi = _unpack_bf16_pair(ybuf_ref[slot, 0])
        y2_lo, y2_hi = _unpack_bf16_pair(ybuf_ref[slot, 1])
        y = jnp.concatenate([w1 * y1_lo + w2 * y2_lo, w1 * y1_hi + w2 * y2_hi], axis=1)
        o_ref[...] = _layer_norm(DEEPNORM_ALPHA * x_ref[...] + y, lng_ref[...], lnb_ref[...])

    @pl.when(i == 0)
    def _():
        request(0, 0)

    for slot in range(2):
        pl.when(lax.rem(i, 2) == slot)(functools.partial(finish, slot))


def _combine_norm(pos, x1, meta, y, ln_g, ln_b):
    tm = TM_MOVE
    t, d = x1.shape
    return pl.pallas_call(
        _combine_kernel,
        out_shape=jax.ShapeDtypeStruct((t, d), F32),
        grid_spec=pltpu.PrefetchScalarGridSpec(
            num_scalar_prefetch=1,
            grid=(t // tm,),
            in_specs=[pl.BlockSpec((tm, d), lambda i, *_: (i, 0)),
                      pl.BlockSpec((tm, LANES), lambda i, *_: (i, 0)),
                      pl.BlockSpec(memory_space=pl.ANY),
                      pl.BlockSpec((1, d), lambda i, *_: (0, 0)),
                      pl.BlockSpec((1, d), lambda i, *_: (0, 0))],
            out_specs=pl.BlockSpec((tm, d), lambda i, *_: (i, 0)),
            scratch_shapes=[pltpu.VMEM((2, 2, tm, HALF), jnp.uint32), pltpu.SemaphoreType.DMA((2,))]),
        compiler_params=_cparams(("arbitrary",)),
        name="moe_combine_norm",
    )(pos, x1, meta, y, ln_g.reshape(1, d), ln_b.reshape(1, d))


def _moe_layer(layer, x1, xp, meta, counts, wg, wu, wd, ln_g, ln_b):
    t = x1.shape[0]
    e12 = meta[:, 0:2].astype(jnp.int32).reshape(2 * t)
    r12 = meta[:, 4:6].astype(jnp.int32).reshape(2 * t)
    cnt = counts[0, :N_EXPERTS].astype(jnp.int32)
    padded = ((cnt + TM_EXPERT - 1) // TM_EXPERT) * TM_EXPERT
    ends = jnp.cumsum(padded)
    off = (ends - padded).astype(jnp.int32)
    pos = off[e12] + r12
    tile_start = jnp.arange(N_EXPERT_TILES, dtype=jnp.int32) * TM_EXPERT
    tile_expert = jnp.sum((tile_start[:, None] >= ends[None, :]).astype(jnp.int32), axis=1)
    tile_expert = jnp.minimum(tile_expert, N_EXPERTS - 1) + layer * N_EXPERTS
    n_used = (ends[-1:] // TM_EXPERT).astype(jnp.int32)
    xs = _scatter_rows(pos, xp)
    y = _expert_mlp(tile_expert, n_used, xs, wg, wu, wd)
    return _combine_norm(pos, x1, meta, y, ln_g, ln_b)


def _router_weights(w_grp, b_grp, w_rt, b_rt):
    d = w_grp.shape[0]
    pad = LANES - N_GROUPS - N_EXPERTS
    wr = jnp.concatenate([w_grp, w_rt.reshape(d, N_EXPERTS), jnp.zeros((d, pad), F32)], axis=1)
    br = jnp.concatenate([b_grp, b_rt.reshape(N_EXPERTS), jnp.zeros((pad,), F32)]).reshape(1, LANES)
    wr_hi = _round_to_bf16(wr)
    wr_lo = _round_to_bf16(wr - wr_hi)
    return wr_hi.astype(BF16), wr_lo.astype(BF16), br


def kernel(x, attn_w_qkv, attn_w_o, conv_w_pw1, conv_b_pw1, conv_w_dw, conv_b_dw, conv_ln_g, conv_ln_b,
           conv_w_pw2, mix_ln_g, mix_ln_b, moe_w_grp, moe_b_grp, moe_w_rt, moe_b_rt, moe_w_gate, moe_w_up,
           moe_w_down, ffn_ln_g, ffn_ln_b):
    b, s, d = x.shape
    assert (b, s, d) == (1, SEQ, D_MODEL)
    xt = x.reshape(s, d)
    ones = jnp.ones((d,), F32)
    zeros = jnp.zeros((d,), F32)

    n_all = DEPTH * N_EXPERTS
    moe_w = (moe_w_gate.reshape(n_all, D_MODEL, D_EXPERT), moe_w_up.reshape(n_all, D_MODEL, D_EXPERT),
             moe_w_down.reshape(n_all, D_EXPERT, D_MODEL))

    qkv = _qkv_proj(xt, attn_w_qkv[0].astype(BF16))
    o = _moba_attention(qkv, _moba_key_table(), _moba_slope_table())
    wr_hi, wr_lo, br = _router_weights(moe_w_grp[0], moe_b_grp[0], moe_w_rt[0], moe_b_rt[0])
    x1, xp, meta, counts = _proj_norm_route(xt, o, attn_w_o[0].astype(BF16), ones, zeros, mix_ln_g[0],
                                            mix_ln_b[0], wr_hi, wr_lo, br, pre_norm=False)
    x2 = _moe_layer(0, x1, xp, meta, counts, *moe_w, ffn_ln_g[0], ffn_ln_b[0])

    g = _pw1_glu(x2, conv_w_pw1[0].astype(BF16), conv_b_pw1[0])
    c = _dwconv(g, conv_w_dw[0], conv_b_dw[0])
    wr_hi, wr_lo, br = _router_weights(moe_w_grp[1], moe_b_grp[1], moe_w_rt[1], moe_b_rt[1])
    x3, xp, meta, counts = _proj_norm_route(x2, c, conv_w_pw2[0].astype(BF16), conv_ln_g[0], conv_ln_b[0],
                                            mix_ln_g[1], mix_ln_b[1], wr_hi, wr_lo, br, pre_norm=True)
    x4 = _moe_layer(1, x3, xp, meta, counts, *moe_w, ffn_ln_g[1], ffn_ln_b[1])
    return x4.reshape(b, s, d)
```

```python
import functools

import jax
import jax.numpy as jnp
import numpy as np
from jax import lax
from jax.experimental import pallas as pl
from jax.experimental.pallas import tpu as pltpu

F32 = jnp.float32
BF16 = jnp.bfloat16

D_MODEL = 2048
SEQ = 16384
DEPTH = 2
N_HEADS = 16
HEAD_DIM = D_MODEL // N_HEADS
MOBA_BLOCK = 256
MOBA_TOPK = 3
N_BLOCKS = SEQ // MOBA_BLOCK
CONV_WIDTH = 31
N_GROUPS = 4
EXPERTS_PER_GROUP = 8
N_EXPERTS = N_GROUPS * EXPERTS_PER_GROUP
D_EXPERT = D_MODEL // 4
LN_EPS = 1e-5
NEG_INF = -1e30
DEEPNORM_ALPHA = (2.0 * DEPTH) ** 0.25

LANES = 128
SUBLANES = 8
CONV_HALO = 32
TM_ROUTE = 512
TM_MOVE = 256
DMA_LOOP_UNROLL = 8
TM_EXPERT = 256
N_EXPERT_TILES = (2 * SEQ) // TM_EXPERT + N_EXPERTS
N_SORTED_ROWS = N_EXPERT_TILES * TM_EXPERT
VMEM_LIMIT = 56 * 1024 * 1024


def _round_to_bf16(v):
    return lax.reduce_precision(v, exponent_bits=8, mantissa_bits=7)


def _cparams(sem):
    return pltpu.CompilerParams(dimension_semantics=sem, vmem_limit_bytes=VMEM_LIMIT)


HALF = D_MODEL // 2
_HI16 = 0xFFFF0000


def _pack_bf16_pair(lo, hi):
    lo_b = lax.bitcast_convert_type(lo.astype(BF16).astype(F32), jnp.uint32)
    hi_b = lax.bitcast_convert_type(hi.astype(BF16).astype(F32), jnp.uint32)
    return (hi_b & jnp.uint32(_HI16)) | (lo_b >> 16)


def _unpack_bf16_pair(u):
    lo = lax.bitcast_convert_type(u << 16, F32)
    hi = lax.bitcast_convert_type(u & jnp.uint32(_HI16), F32)
    return lo, hi


def _qkv_kernel(x_ref, w_ref, o_ref, *, n_scaled_tiles, scale):
    acc = jnp.dot(x_ref[...].astype(BF16), w_ref[...], preferred_element_type=F32)
    s = jnp.where(pl.program_id(1) < n_scaled_tiles, scale, 1.0).astype(F32)
    o_ref[...] = (acc * s).astype(o_ref.dtype)


def _qkv_proj(x, w_bf16):
    tm, tn = 1024, 1024
    t, d = x.shape
    n = w_bf16.shape[1]
    kern = functools.partial(_qkv_kernel, n_scaled_tiles=D_MODEL // tn, scale=HEAD_DIM ** -0.5)
    return pl.pallas_call(
        kern,
        out_shape=jax.ShapeDtypeStruct((t, n), BF16),
        grid=(t // tm, n // tn),
        in_specs=[pl.BlockSpec((tm, d), lambda i, j: (i, 0)),
                  pl.BlockSpec((d, tn), lambda i, j: (0, j))],
        out_specs=pl.BlockSpec((tm, tn), lambda i, j: (i, j)),
        compiler_params=_cparams(("parallel", "arbitrary")),
        name="qkv_proj",
    )(x, w_bf16)


def _glu_kernel(x_ref, wa_ref, wb_ref, ba_ref, bb_ref, o_ref):
    xb = x_ref[...].astype(BF16)
    a = jnp.dot(xb, wa_ref[...], preferred_element_type=F32) + ba_ref[...]
    b = jnp.dot(xb, wb_ref[...], preferred_element_type=F32) + bb_ref[...]
    o_ref[...] = a * jax.nn.sigmoid(b)


def _pw1_glu(x, w_bf16, bias):
    tm, tn = 1024, 512
    t, d = x.shape
    nb = d // tn
    bias2 = bias.reshape(1, 2 * d)
    return pl.pallas_call(
        _glu_kernel,
        out_shape=jax.ShapeDtypeStruct((t, d), F32),
        grid=(t // tm, nb),
        in_specs=[pl.BlockSpec((tm, d), lambda i, j: (i, 0)),
                  pl.BlockSpec((d, tn), lambda i, j: (0, j)),
                  pl.BlockSpec((d, tn), lambda i, j: (0, j + nb)),
                  pl.BlockSpec((1, tn), lambda i, j: (0, j)),
                  pl.BlockSpec((1, tn), lambda i, j: (0, j + nb))],
        out_specs=pl.BlockSpec((tm, tn), lambda i, j: (i, j)),
        compiler_params=_cparams(("parallel", "arbitrary")),
        name="pw1_glu",
    )(x, w_bf16, w_bf16, bias2, bias2)


N_SLOPE_SPLIT = 3
KPOS_LO = 128
HEADS_PER_STEP = 2
CHUNK_BLOCKS = 4
BIAS_LANE0 = N_BLOCKS
N_BIAS_LANES = 2 * N_SLOPE_SPLIT


def _moba_key_table():
    kpos = np.arange(SEQ)
    tab = np.zeros((SEQ, LANES), np.float32)
    tab[kpos, kpos // MOBA_BLOCK] = 1.0
    tab[:, BIAS_LANE0:BIAS_LANE0 + N_SLOPE_SPLIT] = ((kpos // KPOS_LO) * KPOS_LO)[:, None]
    tab[:, BIAS_LANE0 + N_SLOPE_SPLIT:BIAS_LANE0 + N_BIAS_LANES] = (kpos % KPOS_LO)[:, None]
    return jnp.asarray(tab.astype(BF16))


def _moba_slope_table():
    slopes = 2.0 ** (-8.0 * jnp.arange(1, N_HEADS + 1, dtype=F32) / N_HEADS)
    pieces = []
    rem = slopes
    for _ in range(N_SLOPE_SPLIT):
        p = _round_to_bf16(rem)
        pieces.append(p[:, None])
        rem = rem - p
    tab = jnp.concatenate([jnp.zeros((N_HEADS, BIAS_LANE0), F32)] + pieces + pieces
                          + [jnp.zeros((N_HEADS, LANES - BIAS_LANE0 - N_BIAS_LANES), F32)], axis=1)
    tab = jnp.broadcast_to(tab[:, :, None], (N_HEADS, LANES, MOBA_BLOCK))
    return tab.reshape(N_HEADS // HEADS_PER_STEP, HEADS_PER_STEP * LANES, MOBA_BLOCK)


def _moba_kernel(qt_ref, k_ref, vt_ref, c_ref, sl_ref, o_ref, kmean_ref, qa_ref, m_ref, l_ref, acc_ref,
                 s0_ref, s1_ref):
    blk = MOBA_BLOCK
    hp_n = HEADS_PER_STEP
    own = pl.program_id(1)
    hcols = lambda hp: slice(hp * HEAD_DIM, (hp + 1) * HEAD_DIM)

    @pl.when(own == 0)
    def _():
        kmean_ref[...] = jnp.zeros_like(kmean_ref)

        def body(j, carry):
            kb = k_ref[pl.ds(pl.multiple_of(j * blk, blk), blk), :].astype(F32)
            mean = jnp.mean(kb, axis=0, keepdims=True)
            for hp in range(hp_n):
                kmean_ref[hp, pl.ds(j, 1), :] = mean[:, hcols(hp)]
            return carry

        lax.fori_loop(0, N_BLOCKS, body, 0)

    row = lax.broadcasted_iota(jnp.int32, (LANES, blk), 0)
    minus_inf = jnp.float32(-jnp.inf)
    own_rows = pl.ds(pl.multiple_of(own * blk, blk), blk)
    key_i = lax.broadcasted_iota(jnp.int32, (blk, blk), 0)
    qry_i = lax.broadcasted_iota(jnp.int32, (blk, blk), 1)
    hrows = hcols
    for hp in range(hp_n):
        qt = qt_ref[hrows(hp), :]
        km = kmean_ref[hp]
        km_hi = km.astype(BF16)
        km_lo = (km - km_hi.astype(F32)).astype(BF16)
        gate = (jnp.dot(km_hi, qt, preferred_element_type=F32)
                + jnp.dot(km_lo, qt, preferred_element_type=F32))
        gm = jnp.where(row < own, gate, minus_inf)
        sel = row < 0
        for r in range(MOBA_TOPK):
            mx = jnp.max(gm, axis=0, keepdims=True)
            idx = jnp.min(jnp.where(gm == mx, row, LANES), axis=0, keepdims=True)
            hit = row == idx
            sel = sel | (hit & (r < own))
            gm = jnp.where(hit, minus_inf, gm)
        slope_rows = sl_ref[pl.ds(hp * LANES, LANES), :]
        aug = jnp.where(row < N_BLOCKS, jnp.where(sel, 0.0, NEG_INF), slope_rows).astype(BF16)
        qa_ref[hp] = jnp.concatenate([qt, aug], axis=0)

        aug_own = jnp.where(row < N_BLOCKS, 0.0, slope_rows).astype(BF16)
        k_aug = jnp.concatenate([k_ref[own_rows, hcols(hp)], c_ref[own_rows, :]], axis=1)
        s = jnp.dot(k_aug, jnp.concatenate([qt, aug_own], axis=0), preferred_element_type=F32)
        s = jnp.where(key_i <= qry_i, s, NEG_INF)
        m0 = jnp.max(s, axis=0, keepdims=True)
        p = jnp.exp(s - m0)
        m_ref[hp] = m0
        l_ref[hp] = jnp.sum(p, axis=0, keepdims=True)
        acc_ref[hp] = jnp.dot(vt_ref[own, hrows(hp), :], p.astype(BF16), preferred_element_type=F32)

    width = CHUNK_BLOCKS * blk
    last_chunk = SEQ // width - 1

    def score_chunk(c, s_ref):
        rows = pl.ds(pl.multiple_of(c * width, width), width)
        c_rows = c_ref[rows, :]
        for hp in range(hp_n):
            k_aug = jnp.concatenate([k_ref[rows, hcols(hp)], c_rows], axis=1)
            s_ref[hp] = jnp.dot(k_aug, qa_ref[hp], preferred_element_type=F32)

    def absorb_chunk(c, s_ref):
        for hp in range(hp_n):
            s = s_ref[hp]
            m_old = m_ref[hp]
            m_new = jnp.maximum(m_old, jnp.max(s, axis=0, keepdims=True))
            alpha = jnp.exp(m_old - m_new)
            p = jnp.exp(s - m_new)
            l_ref[hp] = alpha * l_ref[hp] + jnp.sum(p, axis=0, keepdims=True)
            p = p.astype(BF16)
            pv = jnp.dot(vt_ref[c * CHUNK_BLOCKS, hrows(hp), :], p[:blk], preferred_element_type=F32)
            for b in range(1, CHUNK_BLOCKS):
                pv = pv + jnp.dot(vt_ref[c * CHUNK_BLOCKS + b, hrows(hp), :], p[b * blk:(b + 1) * blk],
                                  preferred_element_type=F32)
            acc_ref[hp] = alpha * acc_ref[hp] + pv
            m_ref[hp] = m_new

    score_chunk(0, s0_ref)

    def body(i, carry):
        c0 = 2 * i
        score_chunk(c0 + 1, s1_ref)
        absorb_chunk(c0, s0_ref)
        score_chunk(jnp.minimum(c0 + 2, last_chunk), s0_ref)
        absorb_chunk(c0 + 1, s1_ref)
        return carry

    n_pairs = lax.div(own + (2 * CHUNK_BLOCKS - 1), jnp.int32(2 * CHUNK_BLOCKS))
    lax.fori_loop(0, n_pairs, body, 0)
    for hp in range(hp_n):
        o_ref[:, hcols(hp)] = (acc_ref[hp] / l_ref[hp]).T.astype(o_ref.dtype)


def _moba_attention(qkv, key_table, slope_table):
    blk = MOBA_BLOCK
    hp_n = HEADS_PER_STEP
    w = hp_n * HEAD_DIM
    n_groups = N_HEADS // hp_n
    q_t = qkv[:, :D_MODEL].reshape(N_BLOCKS, blk, D_MODEL).transpose(0, 2, 1)
    v_t = qkv[:, 2 * D_MODEL:].reshape(N_BLOCKS, blk, D_MODEL).transpose(0, 2, 1)
    return pl.pallas_call(
        _moba_kernel,
        out_shape=jax.ShapeDtypeStruct((SEQ, D_MODEL), BF16),
        grid=(n_groups, N_BLOCKS),
        in_specs=[pl.BlockSpec((None, w, blk), lambda h, i: (i, h, 0)),
                  pl.BlockSpec((SEQ, w), lambda h, i: (0, n_groups + h)),
                  pl.BlockSpec((N_BLOCKS, w, blk), lambda h, i: (0, h, 0)),
                  pl.BlockSpec((SEQ, LANES), lambda h, i: (0, 0)),
                  pl.BlockSpec((None, hp_n * LANES, blk), lambda h, i: (h, 0, 0))],
        out_specs=pl.BlockSpec((blk, w), lambda h, i: (i, h)),
        scratch_shapes=[pltpu.VMEM((hp_n, LANES, HEAD_DIM), F32),
                        pltpu.VMEM((hp_n, HEAD_DIM + LANES, blk), BF16),
                        pltpu.VMEM((hp_n, 1, blk), F32),
                        pltpu.VMEM((hp_n, 1, blk), F32),
                        pltpu.VMEM((hp_n, HEAD_DIM, blk), F32),
                        pltpu.VMEM((hp_n, CHUNK_BLOCKS * blk, blk), F32),
                        pltpu.VMEM((hp_n, CHUNK_BLOCKS * blk, blk), F32)],
        compiler_params=_cparams(("arbitrary", "arbitrary")),
        name="moba_attention",
    )(q_t, qkv, v_t, key_table, slope_table)


def _dwconv_kernel(cur_ref, prev_ref, w_ref, b_ref, o_ref, win_ref, sh_ref, *, row_chunk):
    tm = cur_ref.shape[0]
    keep = (pl.program_id(0) > 0).astype(F32)
    win_ref[pl.ds(0, CONV_HALO), :] = prev_ref[...] * keep
    win_ref[pl.ds(CONV_HALO, tm), :] = cur_ref[...]
    first = CONV_HALO - (CONV_WIDTH - 1)
    n_sh = sh_ref.shape[1]
    for b in range(1, SUBLANES):
        sh_ref[b - 1] = win_ref[pl.ds(b, n_sh), :]
    for rc in range(tm // row_chunk):
        acc = jnp.broadcast_to(b_ref[...], (row_chunk, cur_ref.shape[1]))
        for k in range(CONV_WIDTH):
            b = (first + k) % SUBLANES
            a = first + k - b + rc * row_chunk
            src = win_ref if b == 0 else sh_ref.at[b - 1]
            acc = acc + w_ref[pl.ds(k, 1), :] * src[pl.ds(a, row_chunk), :]
        o_ref[pl.ds(rc * row_chunk, row_chunk), :] = acc


def _dwconv(g, w_dw, b_dw):
    tm, tc = 512, 512
    t, d = g.shape
    w_pad = jnp.concatenate([w_dw, jnp.zeros((CONV_HALO - CONV_WIDTH, d), F32)], axis=0)
    halo_per_tile = tm // CONV_HALO
    kern = functools.partial(_dwconv_kernel, row_chunk=64)
    return pl.pallas_call(
        kern,
        out_shape=jax.ShapeDtypeStruct((t, d), F32),
        grid=(t // tm, d // tc),
        in_specs=[pl.BlockSpec((tm, tc), lambda i, j: (i, j)),
                  pl.BlockSpec((CONV_HALO, tc), lambda i, j: (jnp.maximum(i * halo_per_tile - 1, 0), j)),
                  pl.BlockSpec((CONV_HALO, tc), lambda i, j: (0, j)),
                  pl.BlockSpec((1, tc), lambda i, j: (0, j))],
        out_specs=pl.BlockSpec((tm, tc), lambda i, j: (i, j)),
        scratch_shapes=[pltpu.VMEM((tm + CONV_HALO, tc), F32),
                        pltpu.VMEM((SUBLANES - 1, tm + CONV_HALO - SUBLANES, tc), F32)],
        compiler_params=_cparams(("parallel", "arbitrary")),
        name="dwconv",
    )(g, g, w_pad, b_dw.reshape(1, d))


def _layer_norm(z, g, b):
    mu = jnp.mean(z, axis=-1, keepdims=True)
    zc = z - mu
    var = jnp.mean(zc * zc, axis=-1, keepdims=True)
    return zc * lax.rsqrt(var + LN_EPS) * g + b


def _route_kernel(xres_ref, a_ref, w_ref, png_ref, pnb_ref, lng_ref, lnb_ref,
                  wr_hi_ref, wr_lo_ref, br_ref,
                  x1_ref, xp_ref, meta_ref, cnt_ref, carry_ref, *, pre_norm):
    tm = xres_ref.shape[0]

    @pl.when(pl.program_id(0) == 0)
    def _():
        carry_ref[...] = jnp.zeros_like(carry_ref)

    a = a_ref[...]
    if pre_norm:
        a = jax.nn.silu(_layer_norm(a.astype(F32), png_ref[...], pnb_ref[...]))
    y = jnp.dot(a.astype(BF16), w_ref[...], preferred_element_type=F32)
    x1 = _layer_norm(DEEPNORM_ALPHA * xres_ref[...] + y, lng_ref[...], lnb_ref[...])
    x1_ref[...] = x1
    xp_ref[...] = _pack_bf16_pair(x1[:, :HALF], x1[:, HALF:])

    x_hi = x1.astype(BF16)
    x_lo = (x1 - x_hi.astype(F32)).astype(BF16)
    logits = (jnp.dot(x_hi, wr_hi_ref[...], preferred_element_type=F32)
              + jnp.dot(x_lo, wr_hi_ref[...], preferred_element_type=F32)
              + jnp.dot(x_hi, wr_lo_ref[...], preferred_element_type=F32)
              + br_ref[...])
    lane = lax.broadcasted_iota(jnp.int32, logits.shape, 1)
    ninf = jnp.float32(-jnp.inf)
    gl = jnp.where(lane < N_GROUPS, logits, ninf)
    gmax = jnp.max(gl, axis=1, keepdims=True)
    g_sel = jnp.min(jnp.where(gl == gmax, lane, LANES), axis=1, keepdims=True)
    g_p = 1.0 / jnp.sum(jnp.exp(gl - gmax), axis=1, keepdims=True)
    lo = N_GROUPS + EXPERTS_PER_GROUP * g_sel
    el = jnp.where((lane >= lo) & (lane < lo + EXPERTS_PER_GROUP), logits, ninf)
    v1 = jnp.max(el, axis=1, keepdims=True)
    i1 = jnp.min(jnp.where(el == v1, lane, LANES), axis=1, keepdims=True)
    el2 = jnp.where(lane == i1, ninf, el)
    v2 = jnp.max(el2, axis=1, keepdims=True)
    i2 = jnp.min(jnp.where(el2 == v2, lane, LANES), axis=1, keepdims=True)
    t = jnp.exp(v2 - v1)
    w1 = g_p / (1.0 + t)
    w2 = g_p * t / (1.0 + t)
    e1 = i1 - N_GROUPS
    e2 = i2 - N_GROUPS

    onehot = ((lane == e1) | (lane == e2)).astype(BF16)
    r_i = lax.broadcasted_iota(jnp.int32, (tm, tm), 0)
    c_i = lax.broadcasted_iota(jnp.int32, (tm, tm), 1)
    lower = (c_i < r_i).astype(BF16)
    before = jnp.dot(lower, onehot, preferred_element_type=F32) + carry_ref[...]
    rank1 = jnp.sum(jnp.where(lane == e1, before, 0.0), axis=1, keepdims=True)
    rank2 = jnp.sum(jnp.where(lane == e2, before, 0.0), axis=1, keepdims=True)
    carry_ref[...] = carry_ref[...] + jnp.sum(onehot.astype(F32), axis=0, keepdims=True)
    cnt_ref[...] = carry_ref[...]

    meta = jnp.where(lane == 0, e1.astype(F32),
           jnp.where(lane == 1, e2.astype(F32),
           jnp.where(lane == 2, w1,
           jnp.where(lane == 3, w2,
           jnp.where(lane == 4, rank1,
           jnp.where(lane == 5, rank2, 0.0))))))
    meta_ref[...] = meta


def _proj_norm_route(xres, a, w_bf16, pn_g, pn_b, ln_g, ln_b, wr_hi, wr_lo, br, *, pre_norm):
    tm = TM_ROUTE
    t, d = xres.shape
    row = lambda i: (i, 0)
    fix = lambda i: (0, 0)
    vec = lambda v: v.reshape(1, d)
    kern = functools.partial(_route_kernel, pre_norm=pre_norm)
    return pl.pallas_call(
        kern,
        out_shape=(jax.ShapeDtypeStruct((t, d), F32),
                   jax.ShapeDtypeStruct((t, HALF), jnp.uint32),
                   jax.ShapeDtypeStruct((t, LANES), F32),
                   jax.ShapeDtypeStruct((1, LANES), F32)),
        grid=(t // tm,),
        in_specs=[pl.BlockSpec((tm, d), row),
                  pl.BlockSpec((tm, d), row),
                  pl.BlockSpec((d, d), fix),
                  pl.BlockSpec((1, d), fix), pl.BlockSpec((1, d), fix),
                  pl.BlockSpec((1, d), fix), pl.BlockSpec((1, d), fix),
                  pl.BlockSpec((d, LANES), fix), pl.BlockSpec((d, LANES), fix),
                  pl.BlockSpec((1, LANES), fix)],
        out_specs=(pl.BlockSpec((tm, d), row),
                   pl.BlockSpec((tm, HALF), row),
                   pl.BlockSpec((tm, LANES), row),
                   pl.BlockSpec((1, LANES), fix)),
        scratch_shapes=[pltpu.VMEM((1, LANES), F32)],
        compiler_params=_cparams(("arbitrary",)),
        name="proj_norm_route",
    )(xres, a, w_bf16, vec(pn_g), vec(pn_b), vec(ln_g), vec(ln_b), wr_hi, wr_lo, br)


def _row_copy(src_ref, src_row, dst_ref, dst_row, sem):
    return pltpu.make_async_copy(src_ref.at[pl.ds(src_row, 1), :], dst_ref.at[pl.ds(dst_row, 1), :], sem)


def _scatter_kernel(pos_ref, x_ref, xs_in_ref, xs_ref, sem):
    del xs_in_ref
    tm = x_ref.shape[0]
    base = 2 * pl.program_id(0) * tm

    def start(tok, carry):
        _row_copy(x_ref, tok, xs_ref, pos_ref[base + 2 * tok], sem).start()
        _row_copy(x_ref, tok, xs_ref, pos_ref[base + 2 * tok + 1], sem).start()
        return carry

    def wait(tok, carry):
        _row_copy(x_ref, tok, xs_ref, pos_ref[base + 2 * tok], sem).wait()
        _row_copy(x_ref, tok, xs_ref, pos_ref[base + 2 * tok + 1], sem).wait()
        return carry

    lax.fori_loop(0, tm, start, 0, unroll=DMA_LOOP_UNROLL)
    lax.fori_loop(0, tm, wait, 0, unroll=DMA_LOOP_UNROLL)


def _scatter_rows(pos, xp):
    tm = TM_MOVE
    t, d = xp.shape
    xs0 = jnp.zeros((N_SORTED_ROWS, d), xp.dtype)
    return pl.pallas_call(
        _scatter_kernel,
        out_shape=jax.ShapeDtypeStruct((N_SORTED_ROWS, d), xp.dtype),
        grid_spec=pltpu.PrefetchScalarGridSpec(
            num_scalar_prefetch=1,
            grid=(t // tm,),
            in_specs=[pl.BlockSpec((tm, d), lambda i, *_: (i, 0)),
                      pl.BlockSpec(memory_space=pl.ANY)],
            out_specs=pl.BlockSpec(memory_space=pl.ANY),
            scratch_shapes=[pltpu.SemaphoreType.DMA]),
        input_output_aliases={2: 0},
        compiler_params=_cparams(("arbitrary",)),
        name="moe_scatter",
    )(pos, xp, xs0)


def _expert_kernel(te_ref, nu_ref, xs_ref, wg_ref, wu_ref, wd_ref, y_ref, wg_bf, wu_bf, wd_bf):
    i = pl.program_id(0)
    used = i < nu_ref[0]
    new_expert = (i == 0) | (te_ref[i] != te_ref[jnp.maximum(i - 1, 0)])

    @pl.when(used & new_expert)
    def _():
        wg_bf[...] = wg_ref[...].astype(BF16)
        wu_bf[...] = wu_ref[...].astype(BF16)
        wd_bf[...] = wd_ref[...].astype(BF16)

    @pl.when(used)
    def _():
        x_lo, x_hi = _unpack_bf16_pair(xs_ref[...])
        x_lo = x_lo.astype(BF16)
        x_hi = x_hi.astype(BF16)
        lo_rows = pl.ds(0, HALF)
        hi_rows = pl.ds(HALF, HALF)
        g = (jnp.dot(x_lo, wg_bf[lo_rows, :], preferred_element_type=F32)
             + jnp.dot(x_hi, wg_bf[hi_rows, :], preferred_element_type=F32))
        u = (jnp.dot(x_lo, wu_bf[lo_rows, :], preferred_element_type=F32)
             + jnp.dot(x_hi, wu_bf[hi_rows, :], preferred_element_type=F32))
        h = (jax.nn.silu(g) * u).astype(BF16)
        y = jnp.dot(h, wd_bf[...], preferred_element_type=F32)
        y_ref[...] = _pack_bf16_pair(y[:, :HALF], y[:, HALF:])

    @pl.when(jnp.logical_not(used))
    def _():
        y_ref[...] = jnp.zeros_like(y_ref)


def _expert_mlp(tile_expert, n_used, xs, wg, wu, wd):
    tm = TM_EXPERT
    d, f = D_MODEL, D_EXPERT
    return pl.pallas_call(
        _expert_kernel,
        out_shape=jax.ShapeDtypeStruct((N_SORTED_ROWS, HALF), jnp.uint32),
        grid_spec=pltpu.PrefetchScalarGridSpec(
            num_scalar_prefetch=2,
            grid=(N_EXPERT_TILES,),
            in_specs=[pl.BlockSpec((tm, HALF), lambda i, te, nu: (i, 0)),
                      pl.BlockSpec((None, d, f), lambda i, te, nu: (te[i], 0, 0)),
                      pl.BlockSpec((None, d, f), lambda i, te, nu: (te[i], 0, 0)),
                      pl.BlockSpec((None, f, d), lambda i, te, nu: (te[i], 0, 0))],
            out_specs=pl.BlockSpec((tm, HALF), lambda i, te, nu: (i, 0)),
            scratch_shapes=[pltpu.VMEM((d, f), BF16), pltpu.VMEM((d, f), BF16), pltpu.VMEM((f, d), BF16)]),
        compiler_params=_cparams(("arbitrary",)),
        name="moe_experts",
    )(tile_expert, n_used, xs, wg, wu, wd)


def _combine_kernel(pos_ref, x_ref, meta_ref, y_ref, lng_ref, lnb_ref, o_ref, ybuf_ref, sems):
    tm = x_ref.shape[0]
    i = pl.program_id(0)
    n_steps = pl.num_programs(0)

    def row_copies(step, slot, tok):
        base = 2 * step * tm
        return (_row_copy(y_ref, pos_ref[base + 2 * tok], ybuf_ref.at[slot, 0], tok, sems.at[slot]),
                _row_copy(y_ref, pos_ref[base + 2 * tok + 1], ybuf_ref.at[slot, 1], tok, sems.at[slot]))

    def request(step, slot):
        def body(tok, carry):
            for cp in row_copies(step, slot, tok):
                cp.start()
            return carry
        lax.fori_loop(0, tm, body, 0, unroll=DMA_LOOP_UNROLL)

    def wait(step, slot):
        def body(tok, carry):
            for cp in row_copies(step, slot, tok):
                cp.wait()
            return carry
        lax.fori_loop(0, tm, body, 0, unroll=DMA_LOOP_UNROLL)

    def finish(slot):
        @pl.when(i + 1 < n_steps)
        def _():
            request(i + 1, 1 - slot)

        wait(i, slot)
        meta = meta_ref[...]
        lane = lax.broadcasted_iota(jnp.int32, meta.shape, 1)
        w1 = jnp.sum(jnp.where(lane == 2, meta, 0.0), axis=1, keepdims=True)
        w2 = jnp.sum(jnp.where(lane == 3, meta, 0.0), axis=1, keepdims=True)
        y1_lo, y1_hi = _unpack_bf16_pair(ybuf_ref[slot, 0])
        y2_lo, y2_hi = _unpack_bf16_pair(ybuf_ref[slot, 1])
        y = jnp.concatenate([w1 * y1_lo + w2 * y2_lo, w1 * y1_hi + w2 * y2_hi], axis=1)
        o_ref[...] = _layer_norm(DEEPNORM_ALPHA * x_ref[...] + y, lng_ref[...], lnb_ref[...])

    @pl.when(i == 0)
    def _():
        request(0, 0)

    for slot in range(2):
        pl.when(lax.rem(i, 2) == slot)(functools.partial(finish, slot))


def _combine_norm(pos, x1, meta, y, ln_g, ln_b):
    tm = TM_MOVE
    t, d = x1.shape
    return pl.pallas_call(
        _combine_kernel,
        out_shape=jax.ShapeDtypeStruct((t, d), F32),
        grid_spec=pltpu.PrefetchScalarGridSpec(
            num_scalar_prefetch=1,
            grid=(t // tm,),
            in_specs=[pl.BlockSpec((tm, d), lambda i, *_: (i, 0)),
                      pl.BlockSpec((tm, LANES), lambda i, *_: (i, 0)),
                      pl.BlockSpec(memory_space=pl.ANY),
                      pl.BlockSpec((1, d), lambda i, *_: (0, 0)),
                      pl.BlockSpec((1, d), lambda i, *_: (0, 0))],
            out_specs=pl.BlockSpec((tm, d), lambda i, *_: (i, 0)),
            scratch_shapes=[pltpu.VMEM((2, 2, tm, HALF), jnp.uint32), pltpu.SemaphoreType.DMA((2,))]),
        compiler_params=_cparams(("arbitrary",)),
        name="moe_combine_norm",
    )(pos, x1, meta, y, ln_g.reshape(1, d), ln_b.reshape(1, d))


def _moe_layer(layer, x1, xp, meta, counts, wg, wu, wd, ln_g, ln_b):
    t = x1.shape[0]
    e12 = meta[:, 0:2].astype(jnp.int32).reshape(2 * t)
    r12 = meta[:, 4:6].astype(jnp.int32).reshape(2 * t)
    cnt = counts[0, :N_EXPERTS].astype(jnp.int32)
    padded = ((cnt + TM_EXPERT - 1) // TM_EXPERT) * TM_EXPERT
    ends = jnp.cumsum(padded)
    off = (ends - padded).astype(jnp.int32)
    pos = off[e12] + r12
    tile_start = jnp.arange(N_EXPERT_TILES, dtype=jnp.int32) * TM_EXPERT
    tile_expert = jnp.sum((tile_start[:, None] >= ends[None, :]).astype(jnp.int32), axis=1)
    tile_expert = jnp.minimum(tile_expert, N_EXPERTS - 1) + layer * N_EXPERTS
    n_used = (ends[-1:] // TM_EXPERT).astype(jnp.int32)
    xs = _scatter_rows(pos, xp)
    y = _expert_mlp(tile_expert, n_used, xs, wg, wu, wd)
    return _combine_norm(pos, x1, meta, y, ln_g, ln_b)


def _router_weights(w_grp, b_grp, w_rt, b_rt):
    d = w_grp.shape[0]
    pad = LANES - N_GROUPS - N_EXPERTS
    wr = jnp.concatenate([w_grp, w_rt.reshape(d, N_EXPERTS), jnp.zeros((d, pad), F32)], axis=1)
    br = jnp.concatenate([b_grp, b_rt.reshape(N_EXPERTS), jnp.zeros((pad,), F32)]).reshape(1, LANES)
    wr_hi = _round_to_bf16(wr)
    wr_lo = _round_to_bf16(wr - wr_hi)
    return wr_hi.astype(BF16), wr_lo.astype(BF16), br


def kernel(x, attn_w_qkv, attn_w_o, conv_w_pw1, conv_b_pw1, conv_w_dw, conv_b_dw, conv_ln_g, conv_ln_b,
           conv_w_pw2, mix_ln_g, mix_ln_b, moe_w_grp, moe_b_grp, moe_w_rt, moe_b_rt, moe_w_gate, moe_w_up,
           moe_w_down, ffn_ln_g, ffn_ln_b):
    b, s, d = x.shape
    assert (b, s, d) == (1, SEQ, D_MODEL)
    xt = x.reshape(s, d)
    ones = jnp.ones((d,), F32)
    zeros = jnp.zeros((d,), F32)

    n_all = DEPTH * N_EXPERTS
    moe_w = (moe_w_gate.reshape(n_all, D_MODEL, D_EXPERT), moe_w_up.reshape(n_all, D_MODEL, D_EXPERT),
             moe_w_down.reshape(n_all, D_EXPERT, D_MODEL))

    qkv = _qkv_proj(xt, attn_w_qkv[0].astype(BF16))
    o = _moba_attention(qkv, _moba_key_table(), _moba_slope_table())
    wr_hi, wr_lo, br = _router_weights(moe_w_grp[0], moe_b_grp[0], moe_w_rt[0], moe_b_rt[0])
    x1, xp, meta, counts = _proj_norm_route(xt, o, attn_w_o[0].astype(BF16), ones, zeros, mix_ln_g[0],
                                            mix_ln_b[0], wr_hi, wr_lo, br, pre_norm=False)
    x2 = _moe_layer(0, x1, xp, meta, counts, *moe_w, ffn_ln_g[0], ffn_ln_b[0])

    g = _pw1_glu(x2, conv_w_pw1[0].astype(BF16), conv_b_pw1[0])
    c = _dwconv(g, conv_w_dw[0], conv_b_dw[0])
    wr_hi, wr_lo, br = _router_weights(moe_w_grp[1], moe_b_grp[1], moe_w_rt[1], moe_b_rt[1])
    x3, xp, meta, counts = _proj_norm_route(x2, c, conv_w_pw2[0].astype(BF16), conv_ln_g[0], conv_ln_b[0],
                                            mix_ln_g[1], mix_ln_b[1], wr_hi, wr_lo, br, pre_norm=True)
    x4 = _moe_layer(1, x3, xp, meta, counts, *moe_w, ffn_ln_g[1], ffn_ln_b[1])
    return x4.reshape(b, s, d)
```

```python
import functools

import jax
import jax.numpy as jnp
import numpy as np
from jax import lax
from jax.experimental import pallas as pl
from jax.experimental.pallas import tpu as pltpu

F32 = jnp.float32
BF16 = jnp.bfloat16

D_MODEL = 2048
SEQ = 16384
DEPTH = 2
N_HEADS = 16
HEAD_DIM = D_MODEL // N_HEADS
MOBA_BLOCK = 256
MOBA_TOPK = 3
N_BLOCKS = SEQ // MOBA_BLOCK
CONV_WIDTH = 31
N_GROUPS = 4
EXPERTS_PER_GROUP = 8
N_EXPERTS = N_GROUPS * EXPERTS_PER_GROUP
D_EXPERT = D_MODEL // 4
LN_EPS = 1e-5
NEG_INF = -1e30
DEEPNORM_ALPHA = (2.0 * DEPTH) ** 0.25

LANES = 128
SUBLANES = 8
CONV_HALO = 32
TM_ROUTE = 512
TM_MOVE = 256
DMA_LOOP_UNROLL = 8
TM_EXPERT = 256
N_EXPERT_TILES = (2 * SEQ) // TM_EXPERT + N_EXPERTS
N_SORTED_ROWS = N_EXPERT_TILES * TM_EXPERT
VMEM_LIMIT = 56 * 1024 * 1024


def _round_to_bf16(v):
    return lax.reduce_precision(v, exponent_bits=8, mantissa_bits=7)


def _cparams(sem):
    return pltpu.CompilerParams(dimension_semantics=sem, vmem_limit_bytes=VMEM_LIMIT)


HALF = D_MODEL // 2
_HI16 = 0xFFFF0000


def _pack_bf16_pair(lo, hi):
    lo_b = lax.bitcast_convert_type(lo.astype(BF16).astype(F32), jnp.uint32)
    hi_b = lax.bitcast_convert_type(hi.astype(BF16).astype(F32), jnp.uint32)
    return (hi_b & jnp.uint32(_HI16)) | (lo_b >> 16)


def _unpack_bf16_pair(u):
    lo = lax.bitcast_convert_type(u << 16, F32)
    hi = lax.bitcast_convert_type(u & jnp.uint32(_HI16), F32)
    return lo, hi


def _qkv_kernel(x_ref, w_ref, o_ref, *, n_scaled_tiles, scale):
    acc = jnp.dot(x_ref[...].astype(BF16), w_ref[...], preferred_element_type=F32)
    s = jnp.where(pl.program_id(1) < n_scaled_tiles, scale, 1.0).astype(F32)
    o_ref[...] = (acc * s).astype(o_ref.dtype)


def _qkv_proj(x, w_bf16):
    tm, tn = 1024, 1024
    t, d = x.shape
    n = w_bf16.shape[1]
    kern = functools.partial(_qkv_kernel, n_scaled_tiles=D_MODEL // tn, scale=HEAD_DIM ** -0.5)
    return pl.pallas_call(
        kern,
        out_shape=jax.ShapeDtypeStruct((t, n), BF16),
        grid=(t // tm, n // tn),
        in_specs=[pl.BlockSpec((tm, d), lambda i, j: (i, 0)),
                  pl.BlockSpec((d, tn), lambda i, j: (0, j))],
        out_specs=pl.BlockSpec((tm, tn), lambda i, j: (i, j)),
        compiler_params=_cparams(("parallel", "arbitrary")),
        name="qkv_proj",
    )(x, w_bf16)


def _glu_kernel(x_ref, wa_ref, wb_ref, ba_ref, bb_ref, o_ref):
    xb = x_ref[...].astype(BF16)
    a = jnp.dot(xb, wa_ref[...], preferred_element_type=F32) + ba_ref[...]
    b = jnp.dot(xb, wb_ref[...], preferred_element_type=F32) + bb_ref[...]
    o_ref[...] = a * jax.nn.sigmoid(b)


def _pw1_glu(x, w_bf16, bias):
    tm, tn = 1024, 512
    t, d = x.shape
    nb = d // tn
    bias2 = bias.reshape(1, 2 * d)
    return pl.pallas_call(
        _glu_kernel,
        out_shape=jax.ShapeDtypeStruct((t, d), F32),
        grid=(t // tm, nb),
        in_specs=[pl.BlockSpec((tm, d), lambda i, j: (i, 0)),
                  pl.BlockSpec((d, tn), lambda i, j: (0, j)),
                  pl.BlockSpec((d, tn), lambda i, j: (0, j + nb)),
                  pl.BlockSpec((1, tn), lambda i, j: (0, j)),
                  pl.BlockSpec((1, tn), lambda i, j: (0, j + nb))],
        out_specs=pl.BlockSpec((tm, tn), lambda i, j: (i, j)),
        compiler_params=_cparams(("parallel", "arbitrary")),
        name="pw1_glu",
    )(x, w_bf16, w_bf16, bias2, bias2)


N_SLOPE_SPLIT = 3
KPOS_LO = 128
HEADS_PER_STEP = 2
CHUNK_BLOCKS = 4
BIAS_LANE0 = N_BLOCKS
N_BIAS_LANES = 2 * N_SLOPE_SPLIT


def _moba_key_table():
    kpos = np.arange(SEQ)
    tab = np.zeros((SEQ, LANES), np.float32)
    tab[kpos, kpos // MOBA_BLOCK] = 1.0
    tab[:, BIAS_LANE0:BIAS_LANE0 + N_SLOPE_SPLIT] = ((kpos // KPOS_LO) * KPOS_LO)[:, None]
    tab[:, BIAS_LANE0 + N_SLOPE_SPLIT:BIAS_LANE0 + N_BIAS_LANES] = (kpos % KPOS_LO)[:, None]
    return jnp.asarray(tab.astype(BF16))


def _moba_slope_table():
    slopes = 2.0 ** (-8.0 * jnp.arange(1, N_HEADS + 1, dtype=F32) / N_HEADS)
    pieces = []
    rem = slopes
    for _ in range(N_SLOPE_SPLIT):
        p = _round_to_bf16(rem)
        pieces.append(p[:, None])
        rem = rem - p
    tab = jnp.concatenate([jnp.zeros((N_HEADS, BIAS_LANE0), F32)] + pieces + pieces
                          + [jnp.zeros((N_HEADS, LANES - BIAS_LANE0 - N_BIAS_LANES), F32)], axis=1)
    tab = jnp.broadcast_to(tab[:, :, None], (N_HEADS, LANES, MOBA_BLOCK))
    return tab.reshape(N_HEADS // HEADS_PER_STEP, HEADS_PER_STEP * LANES, MOBA_BLOCK)


def _moba_kernel(qt_ref, k_ref, vt_ref, c_ref, sl_ref, o_ref, kmean_ref, qa_ref, m_ref, l_ref, acc_ref,
                 s0_ref, s1_ref):
    blk = MOBA_BLOCK
    hp_n = HEADS_PER_STEP
    own = pl.program_id(1)
    hcols = lambda hp: slice(hp * HEAD_DIM, (hp + 1) * HEAD_DIM)

    @pl.when(own == 0)
    def _():
        kmean_ref[...] = jnp.zeros_like(kmean_ref)

        def body(j, carry):
            kb = k_ref[pl.ds(pl.multiple_of(j * blk, blk), blk), :].astype(F32)
            mean = jnp.mean(kb, axis=0, keepdims=True)
            for hp in range(hp_n):
                kmean_ref[hp, pl.ds(j, 1), :] = mean[:, hcols(hp)]
            return carry

        lax.fori_loop(0, N_BLOCKS, body, 0)

    row = lax.broadcasted_iota(jnp.int32, (LANES, blk), 0)
    minus_inf = jnp.float32(-jnp.inf)
    own_rows = pl.ds(pl.multiple_of(own * blk, blk), blk)
    key_i = lax.broadcasted_iota(jnp.int32, (blk, blk), 0)
    qry_i = lax.broadcasted_iota(jnp.int32, (blk, blk), 1)
    hrows = hcols
    for hp in range(hp_n):
        qt = qt_ref[hrows(hp), :]
        km = kmean_ref[hp]
        km_hi = km.astype(BF16)
        km_lo = (km - km_hi.astype(F32)).astype(BF16)
        gate = (jnp.dot(km_hi, qt, preferred_element_type=F32)
                + jnp.dot(km_lo, qt, preferred_element_type=F32))
        gm = jnp.where(row < own, gate, minus_inf)
        sel = row < 0
        for r in range(MOBA_TOPK):
            mx = jnp.max(gm, axis=0, keepdims=True)
            idx = jnp.min(jnp.where(gm == mx, row, LANES), axis=0, keepdims=True)
            hit = row == idx
            sel = sel | (hit & (r < own))
            gm = jnp.where(hit, minus_inf, gm)
        slope_rows = sl_ref[pl.ds(hp * LANES, LANES), :]
        aug = jnp.where(row < N_BLOCKS, jnp.where(sel, 0.0, NEG_INF), slope_rows).astype(BF16)
        qa_ref[hp] = jnp.concatenate([qt, aug], axis=0)

        aug_own = jnp.where(row < N_BLOCKS, 0.0, slope_rows).astype(BF16)
        k_aug = jnp.concatenate([k_ref[own_rows, hcols(hp)], c_ref[own_rows, :]], axis=1)
        s = jnp.dot(k_aug, jnp.concatenate([qt, aug_own], axis=0), preferred_element_type=F32)
        s = jnp.where(key_i <= qry_i, s, NEG_INF)
        m0 = jnp.max(s, axis=0, keepdims=True)
        p = jnp.exp(s - m0)
        m_ref[hp] = m0
        l_ref[hp] = jnp.sum(p, axis=0, keepdims=True)
        acc_ref[hp] = jnp.dot(vt_ref[own, hrows(hp), :], p.astype(BF16), preferred_element_type=F32)

    width = CHUNK_BLOCKS * blk
    last_chunk = SEQ // width - 1

    def score_chunk(c, s_ref):
        rows = pl.ds(pl.multiple_of(c * width, width), width)
        c_rows = c_ref[rows, :]
        for hp in range(hp_n):
            k_aug = jnp.concatenate([k_ref[rows, hcols(hp)], c_rows], axis=1)
            s_ref[hp] = jnp.dot(k_aug, qa_ref[hp], preferred_element_type=F32)

    def absorb_chunk(c, s_ref):
        for hp in range(hp_n):
            s = s_ref[hp]
            m_old = m_ref[hp]
            m_new = jnp.maximum(m_old, jnp.max(s, axis=0, keepdims=True))
            alpha = jnp.exp(m_old - m_new)
            p = jnp.exp(s - m_new)
            l_ref[hp] = alpha * l_ref[hp] + jnp.sum(p, axis=0, keepdims=True)
            p = p.astype(BF16)
            pv = jnp.dot(vt_ref[c * CHUNK_BLOCKS, hrows(hp), :], p[:blk], preferred_element_type=F32)
            for b in range(1, CHUNK_BLOCKS):
                pv = pv + jnp.dot(vt_ref[c * CHUNK_BLOCKS + b, hrows(hp), :], p[b * blk:(b + 1) * blk],
                                  preferred_element_type=F32)
            acc_ref[hp] = alpha * acc_ref[hp] + pv
            m_ref[hp] = m_new

    score_chunk(0, s0_ref)

    def body(i, carry):
        c0 = 2 * i
        score_chunk(c0 + 1, s1_ref)
        absorb_chunk(c0, s0_ref)
        score_chunk(jnp.minimum(c0 + 2, last_chunk), s0_ref)
        absorb_chunk(c0 + 1, s1_ref)
        return carry

    n_chunks = lax.div(own + (CHUNK_BLOCKS - 1), jnp.int32(CHUNK_BLOCKS))
    n_pairs = lax.div(n_chunks, jnp.int32(2))
    lax.fori_loop(0, n_pairs, body, 0)

    @pl.when(n_chunks > 2 * n_pairs)
    def _():
        absorb_chunk(2 * n_pairs, s0_ref)
    for hp in range(hp_n):
        o_ref[:, hcols(hp)] = (acc_ref[hp] / l_ref[hp]).T.astype(o_ref.dtype)


def _moba_attention(qkv, key_table, slope_table):
    blk = MOBA_BLOCK
    hp_n = HEADS_PER_STEP
    w = hp_n * HEAD_DIM
    n_groups = N_HEADS // hp_n
    q_t = qkv[:, :D_MODEL].reshape(N_BLOCKS, blk, D_MODEL).transpose(0, 2, 1)
    v_t = qkv[:, 2 * D_MODEL:].reshape(N_BLOCKS, blk, D_MODEL).transpose(0, 2, 1)
    return pl.pallas_call(
        _moba_kernel,
        out_shape=jax.ShapeDtypeStruct((SEQ, D_MODEL), BF16),
        grid=(n_groups, N_BLOCKS),
        in_specs=[pl.BlockSpec((None, w, blk), lambda h, i: (i, h, 0)),
                  pl.BlockSpec((SEQ, w), lambda h, i: (0, n_groups + h)),
                  pl.BlockSpec((N_BLOCKS, w, blk), lambda h, i: (0, h, 0)),
                  pl.BlockSpec((SEQ, LANES), lambda h, i: (0, 0)),
                  pl.BlockSpec((None, hp_n * LANES, blk), lambda h, i: (h, 0, 0))],
        out_specs=pl.BlockSpec((blk, w), lambda h, i: (i, h)),
        scratch_shapes=[pltpu.VMEM((hp_n, LANES, HEAD_DIM), F32),
                        pltpu.VMEM((hp_n, HEAD_DIM + LANES, blk), BF16),
                        pltpu.VMEM((hp_n, 1, blk), F32),
                        pltpu.VMEM((hp_n, 1, blk), F32),
                        pltpu.VMEM((hp_n, HEAD_DIM, blk), F32),
                        pltpu.VMEM((hp_n, CHUNK_BLOCKS * blk, blk), F32),
                        pltpu.VMEM((hp_n, CHUNK_BLOCKS * blk, blk), F32)],
        compiler_params=_cparams(("arbitrary", "arbitrary")),
        name="moba_attention",
    )(q_t, qkv, v_t, key_table, slope_table)


def _dwconv_kernel(cur_ref, prev_ref, w_ref, b_ref, o_ref, win_ref, sh_ref, *, row_chunk):
    tm = cur_ref.shape[0]
    keep = (pl.program_id(0) > 0).astype(F32)
    win_ref[pl.ds(0, CONV_HALO), :] = prev_ref[...] * keep
    win_ref[pl.ds(CONV_HALO, tm), :] = cur_ref[...]
    first = CONV_HALO - (CONV_WIDTH - 1)
    n_sh = sh_ref.shape[1]
    for b in range(1, SUBLANES):
        sh_ref[b - 1] = win_ref[pl.ds(b, n_sh), :]
    for rc in range(tm // row_chunk):
        acc = jnp.broadcast_to(b_ref[...], (row_chunk, cur_ref.shape[1]))
        for k in range(CONV_WIDTH):
            b = (first + k) % SUBLANES
            a = first + k - b + rc * row_chunk
            src = win_ref if b == 0 else sh_ref.at[b - 1]
            acc = acc + w_ref[pl.ds(k, 1), :] * src[pl.ds(a, row_chunk), :]
        o_ref[pl.ds(rc * row_chunk, row_chunk), :] = acc


def _dwconv(g, w_dw, b_dw):
    tm, tc = 512, 512
    t, d = g.shape
    w_pad = jnp.concatenate([w_dw, jnp.zeros((CONV_HALO - CONV_WIDTH, d), F32)], axis=0)
    halo_per_tile = tm // CONV_HALO
    kern = functools.partial(_dwconv_kernel, row_chunk=64)
    return pl.pallas_call(
        kern,
        out_shape=jax.ShapeDtypeStruct((t, d), F32),
        grid=(t // tm, d // tc),
        in_specs=[pl.BlockSpec((tm, tc), lambda i, j: (i, j)),
                  pl.BlockSpec((CONV_HALO, tc), lambda i, j: (jnp.maximum(i * halo_per_tile - 1, 0), j)),
                  pl.BlockSpec((CONV_HALO, tc), lambda i, j: (0, j)),
                  pl.BlockSpec((1, tc), lambda i, j: (0, j))],
        out_specs=pl.BlockSpec((tm, tc), lambda i, j: (i, j)),
        scratch_shapes=[pltpu.VMEM((tm + CONV_HALO, tc), F32),
                        pltpu.VMEM((SUBLANES - 1, tm + CONV_HALO - SUBLANES, tc), F32)],
        compiler_params=_cparams(("parallel", "arbitrary")),
        name="dwconv",
    )(g, g, w_pad, b_dw.reshape(1, d))


def _layer_norm(z, g, b):
    mu = jnp.mean(z, axis=-1, keepdims=True)
    zc = z - mu
    var = jnp.mean(zc * zc, axis=-1, keepdims=True)
    return zc * lax.rsqrt(var + LN_EPS) * g + b


def _route_kernel(xres_ref, a_ref, w_ref, png_ref, pnb_ref, lng_ref, lnb_ref,
                  wr_hi_ref, wr_lo_ref, br_ref,
                  x1_ref, xp_ref, meta_ref, cnt_ref, carry_ref, *, pre_norm):
    tm = xres_ref.shape[0]

    @pl.when(pl.program_id(0) == 0)
    def _():
        carry_ref[...] = jnp.zeros_like(carry_ref)

    a = a_ref[...]
    if pre_norm:
        a = jax.nn.silu(_layer_norm(a.astype(F32), png_ref[...], pnb_ref[...]))
    y = jnp.dot(a.astype(BF16), w_ref[...], preferred_element_type=F32)
    x1 = _layer_norm(DEEPNORM_ALPHA * xres_ref[...] + y, lng_ref[...], lnb_ref[...])
    x1_ref[...] = x1
    xp_ref[...] = _pack_bf16_pair(x1[:, :HALF], x1[:, HALF:])

    x_hi = x1.astype(BF16)
    x_lo = (x1 - x_hi.astype(F32)).astype(BF16)
    logits = (jnp.dot(x_hi, wr_hi_ref[...], preferred_element_type=F32)
              + jnp.dot(x_lo, wr_hi_ref[...], preferred_element_type=F32)
              + jnp.dot(x_hi, wr_lo_ref[...], preferred_element_type=F32)
              + br_ref[...])
    lane = lax.broadcasted_iota(jnp.int32, logits.shape, 1)
    ninf = jnp.float32(-jnp.inf)
    gl = jnp.where(lane < N_GROUPS, logits, ninf)
    gmax = jnp.max(gl, axis=1, keepdims=True)
    g_sel = jnp.min(jnp.where(gl == gmax, lane, LANES), axis=1, keepdims=True)
    g_p = 1.0 / jnp.sum(jnp.exp(gl - gmax), axis=1, keepdims=True)
    lo = N_GROUPS + EXPERTS_PER_GROUP * g_sel
    el = jnp.where((lane >= lo) & (lane < lo + EXPERTS_PER_GROUP), logits, ninf)
    v1 = jnp.max(el, axis=1, keepdims=True)
    i1 = jnp.min(jnp.where(el == v1, lane, LANES), axis=1, keepdims=True)
    el2 = jnp.where(lane == i1, ninf, el)
    v2 = jnp.max(el2, axis=1, keepdims=True)
    i2 = jnp.min(jnp.where(el2 == v2, lane, LANES), axis=1, keepdims=True)
    t = jnp.exp(v2 - v1)
    w1 = g_p / (1.0 + t)
    w2 = g_p * t / (1.0 + t)
    e1 = i1 - N_GROUPS
    e2 = i2 - N_GROUPS

    onehot = ((lane == e1) | (lane == e2)).astype(BF16)
    r_i = lax.broadcasted_iota(jnp.int32, (tm, tm), 0)
    c_i = lax.broadcasted_iota(jnp.int32, (tm, tm), 1)
    lower = (c_i < r_i).astype(BF16)
    before = jnp.dot(lower, onehot, preferred_element_type=F32) + carry_ref[...]
    rank1 = jnp.sum(jnp.where(lane == e1, before, 0.0), axis=1, keepdims=True)
    rank2 = jnp.sum(jnp.where(lane == e2, before, 0.0), axis=1, keepdims=True)
    carry_ref[...] = carry_ref[...] + jnp.sum(onehot.astype(F32), axis=0, keepdims=True)
    cnt_ref[...] = carry_ref[...]

    meta = jnp.where(lane == 0, e1.astype(F32),
           jnp.where(lane == 1, e2.astype(F32),
           jnp.where(lane == 2, w1,
           jnp.where(lane == 3, w2,
           jnp.where(lane == 4, rank1,
           jnp.where(lane == 5, rank2, 0.0))))))
    meta_ref[...] = meta


def _proj_norm_route(xres, a, w_bf16, pn_g, pn_b, ln_g, ln_b, wr_hi, wr_lo, br, *, pre_norm):
    tm = TM_ROUTE
    t, d = xres.shape
    row = lambda i: (i, 0)
    fix = lambda i: (0, 0)
    vec = lambda v: v.reshape(1, d)
    kern = functools.partial(_route_kernel, pre_norm=pre_norm)
    return pl.pallas_call(
        kern,
        out_shape=(jax.ShapeDtypeStruct((t, d), F32),
                   jax.ShapeDtypeStruct((t, HALF), jnp.uint32),
                   jax.ShapeDtypeStruct((t, LANES), F32),
                   jax.ShapeDtypeStruct((1, LANES), F32)),
        grid=(t // tm,),
        in_specs=[pl.BlockSpec((tm, d), row),
                  pl.BlockSpec((tm, d), row),
                  pl.BlockSpec((d, d), fix),
                  pl.BlockSpec((1, d), fix), pl.BlockSpec((1, d), fix),
                  pl.BlockSpec((1, d), fix), pl.BlockSpec((1, d), fix),
                  pl.BlockSpec((d, LANES), fix), pl.BlockSpec((d, LANES), fix),
                  pl.BlockSpec((1, LANES), fix)],
        out_specs=(pl.BlockSpec((tm, d), row),
                   pl.BlockSpec((tm, HALF), row),
                   pl.BlockSpec((tm, LANES), row),
                   pl.BlockSpec((1, LANES), fix)),
        scratch_shapes=[pltpu.VMEM((1, LANES), F32)],
        compiler_params=_cparams(("arbitrary",)),
        name="proj_norm_route",
    )(xres, a, w_bf16, vec(pn_g), vec(pn_b), vec(ln_g), vec(ln_b), wr_hi, wr_lo, br)


def _row_copy(src_ref, src_row, dst_ref, dst_row, sem):
    return pltpu.make_async_copy(src_ref.at[pl.ds(src_row, 1), :], dst_ref.at[pl.ds(dst_row, 1), :], sem)


def _scatter_kernel(pos_ref, x_ref, xs_in_ref, xs_ref, sem):
    del xs_in_ref
    tm = x_ref.shape[0]
    base = 2 * pl.program_id(0) * tm

    def start(tok, carry):
        _row_copy(x_ref, tok, xs_ref, pos_ref[base + 2 * tok], sem).start()
        _row_copy(x_ref, tok, xs_ref, pos_ref[base + 2 * tok + 1], sem).start()
        return carry

    def wait(tok, carry):
        _row_copy(x_ref, tok, xs_ref, pos_ref[base + 2 * tok], sem).wait()
        _row_copy(x_ref, tok, xs_ref, pos_ref[base + 2 * tok + 1], sem).wait()
        return carry

    lax.fori_loop(0, tm, start, 0, unroll=DMA_LOOP_UNROLL)
    lax.fori_loop(0, tm, wait, 0, unroll=DMA_LOOP_UNROLL)


def _scatter_rows(pos, xp):
    tm = TM_MOVE
    t, d = xp.shape
    xs0 = jnp.zeros((N_SORTED_ROWS, d), xp.dtype)
    return pl.pallas_call(
        _scatter_kernel,
        out_shape=jax.ShapeDtypeStruct((N_SORTED_ROWS, d), xp.dtype),
        grid_spec=pltpu.PrefetchScalarGridSpec(
            num_scalar_prefetch=1,
            grid=(t // tm,),
            in_specs=[pl.BlockSpec((tm, d), lambda i, *_: (i, 0)),
                      pl.BlockSpec(memory_space=pl.ANY)],
            out_specs=pl.BlockSpec(memory_space=pl.ANY),
            scratch_shapes=[pltpu.SemaphoreType.DMA]),
        input_output_aliases={2: 0},
        compiler_params=_cparams(("arbitrary",)),
        name="moe_scatter",
    )(pos, xp, xs0)


def _expert_kernel(te_ref, nu_ref, xs_ref, wg_ref, wu_ref, wd_ref, y_ref, wg_bf, wu_bf, wd_bf):
    i = pl.program_id(0)
    used = i < nu_ref[0]
    new_expert = (i == 0) | (te_ref[i] != te_ref[jnp.maximum(i - 1, 0)])

    @pl.when(used & new_expert)
    def _():
        wg_bf[...] = wg_ref[...].astype(BF16)
        wu_bf[...] = wu_ref[...].astype(BF16)
        wd_bf[...] = wd_ref[...].astype(BF16)

    @pl.when(used)
    def _():
        x_lo, x_hi = _unpack_bf16_pair(xs_ref[...])
        x_lo = x_lo.astype(BF16)
        x_hi = x_hi.astype(BF16)
        lo_rows = pl.ds(0, HALF)
        hi_rows = pl.ds(HALF, HALF)
        g = (jnp.dot(x_lo, wg_bf[lo_rows, :], preferred_element_type=F32)
             + jnp.dot(x_hi, wg_bf[hi_rows, :], preferred_element_type=F32))
        u = (jnp.dot(x_lo, wu_bf[lo_rows, :], preferred_element_type=F32)
             + jnp.dot(x_hi, wu_bf[hi_rows, :], preferred_element_type=F32))
        h = (jax.nn.silu(g) * u).astype(BF16)
        y = jnp.dot(h, wd_bf[...], preferred_element_type=F32)
        y_ref[...] = _pack_bf16_pair(y[:, :HALF], y[:, HALF:])

    @pl.when(jnp.logical_not(used))
    def _():
        y_ref[...] = jnp.zeros_like(y_ref)


def _expert_mlp(tile_expert, n_used, xs, wg, wu, wd):
    tm = TM_EXPERT
    d, f = D_MODEL, D_EXPERT
    return pl.pallas_call(
        _expert_kernel,
        out_shape=jax.ShapeDtypeStruct((N_SORTED_ROWS, HALF), jnp.uint32),
        grid_spec=pltpu.PrefetchScalarGridSpec(
            num_scalar_prefetch=2,
            grid=(N_EXPERT_TILES,),
            in_specs=[pl.BlockSpec((tm, HALF), lambda i, te, nu: (i, 0)),
                      pl.BlockSpec((None, d, f), lambda i, te, nu: (te[i], 0, 0)),
                      pl.BlockSpec((None, d, f), lambda i, te, nu: (te[i], 0, 0)),
                      pl.BlockSpec((None, f, d), lambda i, te, nu: (te[i], 0, 0))],
            out_specs=pl.BlockSpec((tm, HALF), lambda i, te, nu: (i, 0)),
            scratch_shapes=[pltpu.VMEM((d, f), BF16), pltpu.VMEM((d, f), BF16), pltpu.VMEM((f, d), BF16)]),
        compiler_params=_cparams(("arbitrary",)),
        name="moe_experts",
    )(tile_expert, n_used, xs, wg, wu, wd)


def _combine_kernel(pos_ref, x_ref, meta_ref, y_ref, lng_ref, lnb_ref, o_ref, ybuf_ref, sems):
    tm = x_ref.shape[0]
    i = pl.program_id(0)
    n_steps = pl.num_programs(0)

    def row_copies(step, slot, tok):
        base = 2 * step * tm
        return (_row_copy(y_ref, pos_ref[base + 2 * tok], ybuf_ref.at[slot, 0], tok, sems.at[slot]),
                _row_copy(y_ref, pos_ref[base + 2 * tok + 1], ybuf_ref.at[slot, 1], tok, sems.at[slot]))

    def request(step, slot):
        def body(tok, carry):
            for cp in row_copies(step, slot, tok):
                cp.start()
            return carry
        lax.fori_loop(0, tm, body, 0, unroll=DMA_LOOP_UNROLL)

    def wait(step, slot):
        def body(tok, carry):
            for cp in row_copies(step, slot, tok):
                cp.wait()
            return carry
        lax.fori_loop(0, tm, body, 0, unroll=DMA_LOOP_UNROLL)

    def finish(slot):
        @pl.when(i + 1 < n_steps)
        def _():
            request(i + 1, 1 - slot)

        wait(i, slot)
        meta = meta_ref[...]
        lane = lax.broadcasted_iota(jnp.int32, meta.shape, 1)
        w1 = jnp.sum(jnp.where(lane == 2, meta, 0.0), axis=1, keepdims=True)
        w2 = jnp.sum(jnp.where(lane == 3, meta, 0.0), axis=1, keepdims=True)
        y1_lo, y1_hi = _unpack_bf16_pair(ybuf_ref[slot, 0])
        y2_lo, y2_hi = _unpack_bf16_pair(ybuf_ref[slot, 1])
        y = jnp.concatenate([w1 * y1_lo + w2 * y2_lo, w1 * y1_hi + w2 * y2_hi], axis=1)
        o_ref[...] = _layer_norm(DEEPNORM_ALPHA * x_ref[...] + y, lng_ref[...], lnb_ref[...])

    @pl.when(i == 0)
    def _():
        request(0, 0)

    for slot in range(2):
        pl.when(lax.rem(i, 2) == slot)(functools.partial(finish, slot))


def _combine_norm(pos, x1, meta, y, ln_g, ln_b):
    tm = TM_MOVE
    t, d = x1.shape
    return pl.pallas_call(
        _combine_kernel,
        out_shape=jax.ShapeDtypeStruct((t, d), F32),
        grid_spec=pltpu.PrefetchScalarGridSpec(
            num_scalar_prefetch=1,
            grid=(t // tm,),
            in_specs=[pl.BlockSpec((tm, d), lambda i, *_: (i, 0)),
                      pl.BlockSpec((tm, LANES), lambda i, *_: (i, 0)),
                      pl.BlockSpec(memory_space=pl.ANY),
                      pl.BlockSpec((1, d), lambda i, *_: (0, 0)),
                      pl.BlockSpec((1, d), lambda i, *_: (0, 0))],
            out_specs=pl.BlockSpec((tm, d), lambda i, *_: (i, 0)),
            scratch_shapes=[pltpu.VMEM((2, 2, tm, HALF), jnp.uint32), pltpu.SemaphoreType.DMA((2,))]),
        compiler_params=_cparams(("arbitrary",)),
        name="moe_combine_norm",
    )(pos, x1, meta, y, ln_g.reshape(1, d), ln_b.reshape(1, d))


def _moe_layer(layer, x1, xp, meta, counts, wg, wu, wd, ln_g, ln_b):
    t = x1.shape[0]
    e12 = meta[:, 0:2].astype(jnp.int32).reshape(2 * t)
    r12 = meta[:, 4:6].astype(jnp.int32).reshape(2 * t)
    cnt = counts[0, :N_EXPERTS].astype(jnp.int32)
    padded = ((cnt + TM_EXPERT - 1) // TM_EXPERT) * TM_EXPERT
    ends = jnp.cumsum(padded)
    off = (ends - padded).astype(jnp.int32)
    pos = off[e12] + r12
    tile_start = jnp.arange(N_EXPERT_TILES, dtype=jnp.int32) * TM_EXPERT
    tile_expert = jnp.sum((tile_start[:, None] >= ends[None, :]).astype(jnp.int32), axis=1)
    tile_expert = jnp.minimum(tile_expert, N_EXPERTS - 1) + layer * N_EXPERTS
    n_used = (ends[-1:] // TM_EXPERT).astype(jnp.int32)
    xs = _scatter_rows(pos, xp)
    y = _expert_mlp(tile_expert, n_used, xs, wg, wu, wd)
    return _combine_norm(pos, x1, meta, y, ln_g, ln_b)


def _router_weights(w_grp, b_grp, w_rt, b_rt):
    d = w_grp.shape[0]
    pad = LANES - N_GROUPS - N_EXPERTS
    wr = jnp.concatenate([w_grp, w_rt.reshape(d, N_EXPERTS), jnp.zeros((d, pad), F32)], axis=1)
    br = jnp.concatenate([b_grp, b_rt.reshape(N_EXPERTS), jnp.zeros((pad,), F32)]).reshape(1, LANES)
    wr_hi = _round_to_bf16(wr)
    wr_lo = _round_to_bf16(wr - wr_hi)
    return wr_hi.astype(BF16), wr_lo.astype(BF16), br


def kernel(x, attn_w_qkv, attn_w_o, conv_w_pw1, conv_b_pw1, conv_w_dw, conv_b_dw, conv_ln_g, conv_ln_b,
           conv_w_pw2, mix_ln_g, mix_ln_b, moe_w_grp, moe_b_grp, moe_w_rt, moe_b_rt, moe_w_gate, moe_w_up,
           moe_w_down, ffn_ln_g, ffn_ln_b):
    b, s, d = x.shape
    assert (b, s, d) == (1, SEQ, D_MODEL)
    xt = x.reshape(s, d)
    ones = jnp.ones((d,), F32)
    zeros = jnp.zeros((d,), F32)

    n_all = DEPTH * N_EXPERTS
    moe_w = (moe_w_gate.reshape(n_all, D_MODEL, D_EXPERT), moe_w_up.reshape(n_all, D_MODEL, D_EXPERT),
             moe_w_down.reshape(n_all, D_EXPERT, D_MODEL))

    qkv = _qkv_proj(xt, attn_w_qkv[0].astype(BF16))
    o = _moba_attention(qkv, _moba_key_table(), _moba_slope_table())
    wr_hi, wr_lo, br = _router_weights(moe_w_grp[0], moe_b_grp[0], moe_w_rt[0], moe_b_rt[0])
    x1, xp, meta, counts = _proj_norm_route(xt, o, attn_w_o[0].astype(BF16), ones, zeros, mix_ln_g[0],
                                            mix_ln_b[0], wr_hi, wr_lo, br, pre_norm=False)
    x2 = _moe_layer(0, x1, xp, meta, counts, *moe_w, ffn_ln_g[0], ffn_ln_b[0])

    g = _pw1_glu(x2, conv_w_pw1[0].astype(BF16), conv_b_pw1[0])
    c = _dwconv(g, conv_w_dw[0], conv_b_dw[0])
    wr_hi, wr_lo, br = _router_weights(moe_w_grp[1], moe_b_grp[1], moe_w_rt[1], moe_b_rt[1])
    x3, xp, meta, counts = _proj_norm_route(x2, c, conv_w_pw2[0].astype(BF16), conv_ln_g[0], conv_ln_b[0],
                                            mix_ln_g[1], mix_ln_b[1], wr_hi, wr_lo, br, pre_norm=True)
    x4 = _moe_layer(1, x3, xp, meta, counts, *moe_w, ffn_ln_g[1], ffn_ln_b[1])
    return x4.reshape(b, s, d)
```

```python
import functools

import jax
import jax.numpy as jnp
import numpy as np
from jax import lax
from jax.experimental import pallas as pl
from jax.experimental.pallas import tpu as pltpu

F32 = jnp.float32
BF16 = jnp.bfloat16

D_MODEL = 2048
SEQ = 16384
DEPTH = 2
N_HEADS = 16
HEAD_DIM = D_MODEL // N_HEADS
MOBA_BLOCK = 256
MOBA_TOPK = 3
N_BLOCKS = SEQ // MOBA_BLOCK
CONV_WIDTH = 31
N_GROUPS = 4
EXPERTS_PER_GROUP = 8
N_EXPERTS = N_GROUPS * EXPERTS_PER_GROUP
D_EXPERT = D_MODEL // 4
LN_EPS = 1e-5
NEG_INF = -1e30
DEEPNORM_ALPHA = (2.0 * DEPTH) ** 0.25

LANES = 128
SUBLANES = 8
CONV_HALO = 32
TM_ROUTE = 512
TM_MOVE = 256
DMA_LOOP_UNROLL = 8
TM_EXPERT = 256
N_EXPERT_TILES = (2 * SEQ) // TM_EXPERT + N_EXPERTS
N_SORTED_ROWS = N_EXPERT_TILES * TM_EXPERT
VMEM_LIMIT = 56 * 1024 * 1024


def _round_to_bf16(v):
    return lax.reduce_precision(v, exponent_bits=8, mantissa_bits=7)


def _cparams(sem):
    return pltpu.CompilerParams(dimension_semantics=sem, vmem_limit_bytes=VMEM_LIMIT)


HALF = D_MODEL // 2
_HI16 = 0xFFFF0000


def _pack_bf16_pair(lo, hi):
    lo_b = lax.bitcast_convert_type(lo.astype(BF16).astype(F32), jnp.uint32)
    hi_b = lax.bitcast_convert_type(hi.astype(BF16).astype(F32), jnp.uint32)
    return (hi_b & jnp.uint32(_HI16)) | (lo_b >> 16)


def _unpack_bf16_pair(u):
    lo = lax.bitcast_convert_type(u << 16, F32)
    hi = lax.bitcast_convert_type(u & jnp.uint32(_HI16), F32)
    return lo, hi


def _qkv_kernel(x_ref, w_ref, o_ref, *, n_scaled_tiles, scale):
    acc = jnp.dot(x_ref[...].astype(BF16), w_ref[...], preferred_element_type=F32)
    s = jnp.where(pl.program_id(1) < n_scaled_tiles, scale, 1.0).astype(F32)
    o_ref[...] = (acc * s).astype(o_ref.dtype)


def _qkv_proj(x, w_bf16):
    tm, tn = 1024, 1024
    t, d = x.shape
    n = w_bf16.shape[1]
    kern = functools.partial(_qkv_kernel, n_scaled_tiles=D_MODEL // tn, scale=HEAD_DIM ** -0.5)
    return pl.pallas_call(
        kern,
        out_shape=jax.ShapeDtypeStruct((t, n), BF16),
        grid=(t // tm, n // tn),
        in_specs=[pl.BlockSpec((tm, d), lambda i, j: (i, 0)),
                  pl.BlockSpec((d, tn), lambda i, j: (0, j))],
        out_specs=pl.BlockSpec((tm, tn), lambda i, j: (i, j)),
        compiler_params=_cparams(("parallel", "arbitrary")),
        name="qkv_proj",
    )(x, w_bf16)


def _glu_kernel(x_ref, wa_ref, wb_ref, ba_ref, bb_ref, o_ref):
    xb = x_ref[...].astype(BF16)
    a = jnp.dot(xb, wa_ref[...], preferred_element_type=F32) + ba_ref[...]
    b = jnp.dot(xb, wb_ref[...], preferred_element_type=F32) + bb_ref[...]
    o_ref[...] = a * jax.nn.sigmoid(b)


def _pw1_glu(x, w_bf16, bias):
    tm, tn = 1024, 512
    t, d = x.shape
    nb = d // tn
    bias2 = bias.reshape(1, 2 * d)
    return pl.pallas_call(
        _glu_kernel,
        out_shape=jax.ShapeDtypeStruct((t, d), F32),
        grid=(t // tm, nb),
        in_specs=[pl.BlockSpec((tm, d), lambda i, j: (i, 0)),
                  pl.BlockSpec((d, tn), lambda i, j: (0, j)),
                  pl.BlockSpec((d, tn), lambda i, j: (0, j + nb)),
                  pl.BlockSpec((1, tn), lambda i, j: (0, j)),
                  pl.BlockSpec((1, tn), lambda i, j: (0, j + nb))],
        out_specs=pl.BlockSpec((tm, tn), lambda i, j: (i, j)),
        compiler_params=_cparams(("parallel", "arbitrary")),
        name="pw1_glu",
    )(x, w_bf16, w_bf16, bias2, bias2)


N_SLOPE_SPLIT = 3
KPOS_LO = 128
HEADS_PER_STEP = 2
CHUNK_BLOCKS = 8
BIAS_LANE0 = N_BLOCKS
N_BIAS_LANES = 2 * N_SLOPE_SPLIT


def _moba_key_table():
    kpos = np.arange(SEQ)
    tab = np.zeros((SEQ, LANES), np.float32)
    tab[kpos, kpos // MOBA_BLOCK] = 1.0
    tab[:, BIAS_LANE0:BIAS_LANE0 + N_SLOPE_SPLIT] = ((kpos // KPOS_LO) * KPOS_LO)[:, None]
    tab[:, BIAS_LANE0 + N_SLOPE_SPLIT:BIAS_LANE0 + N_BIAS_LANES] = (kpos % KPOS_LO)[:, None]
    return jnp.asarray(tab.astype(BF16))


def _moba_slope_table():
    slopes = 2.0 ** (-8.0 * jnp.arange(1, N_HEADS + 1, dtype=F32) / N_HEADS)
    pieces = []
    rem = slopes
    for _ in range(N_SLOPE_SPLIT):
        p = _round_to_bf16(rem)
        pieces.append(p[:, None])
        rem = rem - p
    tab = jnp.concatenate([jnp.zeros((N_HEADS, BIAS_LANE0), F32)] + pieces + pieces
                          + [jnp.zeros((N_HEADS, LANES - BIAS_LANE0 - N_BIAS_LANES), F32)], axis=1)
    tab = jnp.broadcast_to(tab[:, :, None], (N_HEADS, LANES, MOBA_BLOCK))
    return tab.reshape(N_HEADS // HEADS_PER_STEP, HEADS_PER_STEP * LANES, MOBA_BLOCK)


def _moba_kernel(qt_ref, k_ref, vt_ref, c_ref, sl_ref, o_ref, kmean_ref, qa_ref, m_ref, l_ref, acc_ref,
                 s0_ref, s1_ref):
    blk = MOBA_BLOCK
    hp_n = HEADS_PER_STEP
    own = pl.program_id(1)
    hcols = lambda hp: slice(hp * HEAD_DIM, (hp + 1) * HEAD_DIM)

    @pl.when(own == 0)
    def _():
        kmean_ref[...] = jnp.zeros_like(kmean_ref)

        def body(j, carry):
            kb = k_ref[pl.ds(pl.multiple_of(j * blk, blk), blk), :].astype(F32)
            mean = jnp.mean(kb, axis=0, keepdims=True)
            for hp in range(hp_n):
                kmean_ref[hp, pl.ds(j, 1), :] = mean[:, hcols(hp)]
            return carry

        lax.fori_loop(0, N_BLOCKS, body, 0)

    row = lax.broadcasted_iota(jnp.int32, (LANES, blk), 0)
    minus_inf = jnp.float32(-jnp.inf)
    own_rows = pl.ds(pl.multiple_of(own * blk, blk), blk)
    key_i = lax.broadcasted_iota(jnp.int32, (blk, blk), 0)
    qry_i = lax.broadcasted_iota(jnp.int32, (blk, blk), 1)
    hrows = hcols
    for hp in range(hp_n):
        qt = qt_ref[hrows(hp), :]
        km = kmean_ref[hp]
        km_hi = km.astype(BF16)
        km_lo = (km - km_hi.astype(F32)).astype(BF16)
        gate = (jnp.dot(km_hi, qt, preferred_element_type=F32)
                + jnp.dot(km_lo, qt, preferred_element_type=F32))
        gm = jnp.where(row < own, gate, minus_inf)
        sel = row < 0
        for r in range(MOBA_TOPK):
            mx = jnp.max(gm, axis=0, keepdims=True)
            idx = jnp.min(jnp.where(gm == mx, row, LANES), axis=0, keepdims=True)
            hit = row == idx
            sel = sel | (hit & (r < own))
            gm = jnp.where(hit, minus_inf, gm)
        slope_rows = sl_ref[pl.ds(hp * LANES, LANES), :]
        aug = jnp.where(row < N_BLOCKS, jnp.where(sel, 0.0, NEG_INF), slope_rows).astype(BF16)
        qa_ref[hp] = jnp.concatenate([qt, aug], axis=0)

        aug_own = jnp.where(row < N_BLOCKS, 0.0, slope_rows).astype(BF16)
        k_aug = jnp.concatenate([k_ref[own_rows, hcols(hp)], c_ref[own_rows, :]], axis=1)
        s = jnp.dot(k_aug, jnp.concatenate([qt, aug_own], axis=0), preferred_element_type=F32)
        s = jnp.where(key_i <= qry_i, s, NEG_INF)
        m0 = jnp.max(s, axis=0, keepdims=True)
        p = jnp.exp(s - m0)
        m_ref[hp] = m0
        l_ref[hp] = jnp.sum(p, axis=0, keepdims=True)
        acc_ref[hp] = jnp.dot(vt_ref[own, hrows(hp), :], p.astype(BF16), preferred_element_type=F32)

    width = CHUNK_BLOCKS * blk
    last_chunk = SEQ // width - 1

    def score_chunk(c, s_ref):
        rows = pl.ds(pl.multiple_of(c * width, width), width)
        c_rows = c_ref[rows, :]
        for hp in range(hp_n):
            k_aug = jnp.concatenate([k_ref[rows, hcols(hp)], c_rows], axis=1)
            s_ref[hp] = jnp.dot(k_aug, qa_ref[hp], preferred_element_type=F32)

    def absorb_chunk(c, s_ref):
        for hp in range(hp_n):
            s = s_ref[hp]
            m_old = m_ref[hp]
            m_new = jnp.maximum(m_old, jnp.max(s, axis=0, keepdims=True))
            alpha = jnp.exp(m_old - m_new)
            p = jnp.exp(s - m_new)
            l_ref[hp] = alpha * l_ref[hp] + jnp.sum(p, axis=0, keepdims=True)
            p = p.astype(BF16)
            pv = jnp.dot(vt_ref[c * CHUNK_BLOCKS, hrows(hp), :], p[:blk], preferred_element_type=F32)
            for b in range(1, CHUNK_BLOCKS):
                pv = pv + jnp.dot(vt_ref[c * CHUNK_BLOCKS + b, hrows(hp), :], p[b * blk:(b + 1) * blk],
                                  preferred_element_type=F32)
            acc_ref[hp] = alpha * acc_ref[hp] + pv
            m_ref[hp] = m_new

    score_chunk(0, s0_ref)

    def body(i, carry):
        c0 = 2 * i
        score_chunk(c0 + 1, s1_ref)
        absorb_chunk(c0, s0_ref)
        score_chunk(jnp.minimum(c0 + 2, last_chunk), s0_ref)
        absorb_chunk(c0 + 1, s1_ref)
        return carry

    n_chunks = lax.div(own + (CHUNK_BLOCKS - 1), jnp.int32(CHUNK_BLOCKS))
    n_pairs = lax.div(n_chunks, jnp.int32(2))
    lax.fori_loop(0, n_pairs, body, 0)

    @pl.when(n_chunks > 2 * n_pairs)
    def _():
        absorb_chunk(2 * n_pairs, s0_ref)
    for hp in range(hp_n):
        o_ref[:, hcols(hp)] = (acc_ref[hp] / l_ref[hp]).T.astype(o_ref.dtype)


def _moba_attention(qkv, key_table, slope_table):
    blk = MOBA_BLOCK
    hp_n = HEADS_PER_STEP
    w = hp_n * HEAD_DIM
    n_groups = N_HEADS // hp_n
    q_t = qkv[:, :D_MODEL].reshape(N_BLOCKS, blk, D_MODEL).transpose(0, 2, 1)
    v_t = qkv[:, 2 * D_MODEL:].reshape(N_BLOCKS, blk, D_MODEL).transpose(0, 2, 1)
    return pl.pallas_call(
        _moba_kernel,
        out_shape=jax.ShapeDtypeStruct((SEQ, D_MODEL), BF16),
        grid=(n_groups, N_BLOCKS),
        in_specs=[pl.BlockSpec((None, w, blk), lambda h, i: (i, h, 0)),
                  pl.BlockSpec((SEQ, w), lambda h, i: (0, n_groups + h)),
                  pl.BlockSpec((N_BLOCKS, w, blk), lambda h, i: (0, h, 0)),
                  pl.BlockSpec((SEQ, LANES), lambda h, i: (0, 0)),
                  pl.BlockSpec((None, hp_n * LANES, blk), lambda h, i: (h, 0, 0))],
        out_specs=pl.BlockSpec((blk, w), lambda h, i: (i, h)),
        scratch_shapes=[pltpu.VMEM((hp_n, LANES, HEAD_DIM), F32),
                        pltpu.VMEM((hp_n, HEAD_DIM + LANES, blk), BF16),
                        pltpu.VMEM((hp_n, 1, blk), F32),
                        pltpu.VMEM((hp_n, 1, blk), F32),
                        pltpu.VMEM((hp_n, HEAD_DIM, blk), F32),
                        pltpu.VMEM((hp_n, CHUNK_BLOCKS * blk, blk), F32),
                        pltpu.VMEM((hp_n, CHUNK_BLOCKS * blk, blk), F32)],
        compiler_params=_cparams(("arbitrary", "arbitrary")),
        name="moba_attention",
    )(q_t, qkv, v_t, key_table, slope_table)


def _dwconv_kernel(cur_ref, prev_ref, w_ref, b_ref, o_ref, win_ref, sh_ref, *, row_chunk):
    tm = cur_ref.shape[0]
    keep = (pl.program_id(0) > 0).astype(F32)
    win_ref[pl.ds(0, CONV_HALO), :] = prev_ref[...] * keep
    win_ref[pl.ds(CONV_HALO, tm), :] = cur_ref[...]
    first = CONV_HALO - (CONV_WIDTH - 1)
    n_sh = sh_ref.shape[1]
    for b in range(1, SUBLANES):
        sh_ref[b - 1] = win_ref[pl.ds(b, n_sh), :]
    for rc in range(tm // row_chunk):
        acc = jnp.broadcast_to(b_ref[...], (row_chunk, cur_ref.shape[1]))
        for k in range(CONV_WIDTH):
            b = (first + k) % SUBLANES
            a = first + k - b + rc * row_chunk
            src = win_ref if b == 0 else sh_ref.at[b - 1]
            acc = acc + w_ref[pl.ds(k, 1), :] * src[pl.ds(a, row_chunk), :]
        o_ref[pl.ds(rc * row_chunk, row_chunk), :] = acc


def _dwconv(g, w_dw, b_dw):
    tm, tc = 512, 512
    t, d = g.shape
    w_pad = jnp.concatenate([w_dw, jnp.zeros((CONV_HALO - CONV_WIDTH, d), F32)], axis=0)
    halo_per_tile = tm // CONV_HALO
    kern = functools.partial(_dwconv_kernel, row_chunk=64)
    return pl.pallas_call(
        kern,
        out_shape=jax.ShapeDtypeStruct((t, d), F32),
        grid=(t // tm, d // tc),
        in_specs=[pl.BlockSpec((tm, tc), lambda i, j: (i, j)),
                  pl.BlockSpec((CONV_HALO, tc), lambda i, j: (jnp.maximum(i * halo_per_tile - 1, 0), j)),
                  pl.BlockSpec((CONV_HALO, tc), lambda i, j: (0, j)),
                  pl.BlockSpec((1, tc), lambda i, j: (0, j))],
        out_specs=pl.BlockSpec((tm, tc), lambda i, j: (i, j)),
        scratch_shapes=[pltpu.VMEM((tm + CONV_HALO, tc), F32),
                        pltpu.VMEM((SUBLANES - 1, tm + CONV_HALO - SUBLANES, tc), F32)],
        compiler_params=_cparams(("parallel", "arbitrary")),
        name="dwconv",
    )(g, g, w_pad, b_dw.reshape(1, d))


def _layer_norm(z, g, b):
    mu = jnp.mean(z, axis=-1, keepdims=True)
    zc = z - mu
    var = jnp.mean(zc * zc, axis=-1, keepdims=True)
    return zc * lax.rsqrt(var + LN_EPS) * g + b


def _route_kernel(xres_ref, a_ref, w_ref, png_ref, pnb_ref, lng_ref, lnb_ref,
                  wr_hi_ref, wr_lo_ref, br_ref,
                  x1_ref, xp_ref, meta_ref, cnt_ref, carry_ref, *, pre_norm):
    tm = xres_ref.shape[0]

    @pl.when(pl.program_id(0) == 0)
    def _():
        carry_ref[...] = jnp.zeros_like(carry_ref)

    a = a_ref[...]
    if pre_norm:
        a = jax.nn.silu(_layer_norm(a.astype(F32), png_ref[...], pnb_ref[...]))
    y = jnp.dot(a.astype(BF16), w_ref[...], preferred_element_type=F32)
    x1 = _layer_norm(DEEPNORM_ALPHA * xres_ref[...] + y, lng_ref[...], lnb_ref[...])
    x1_ref[...] = x1
    xp_ref[...] = _pack_bf16_pair(x1[:, :HALF], x1[:, HALF:])

    x_hi = x1.astype(BF16)
    x_lo = (x1 - x_hi.astype(F32)).astype(BF16)
    logits = (jnp.dot(x_hi, wr_hi_ref[...], preferred_element_type=F32)
              + jnp.dot(x_lo, wr_hi_ref[...], preferred_element_type=F32)
              + jnp.dot(x_hi, wr_lo_ref[...], preferred_element_type=F32)
              + br_ref[...])
    lane = lax.broadcasted_iota(jnp.int32, logits.shape, 1)
    ninf = jnp.float32(-jnp.inf)
    gl = jnp.where(lane < N_GROUPS, logits, ninf)
    gmax = jnp.max(gl, axis=1, keepdims=True)
    g_sel = jnp.min(jnp.where(gl == gmax, lane, LANES), axis=1, keepdims=True)
    g_p = 1.0 / jnp.sum(jnp.exp(gl - gmax), axis=1, keepdims=True)
    lo = N_GROUPS + EXPERTS_PER_GROUP * g_sel
    el = jnp.where((lane >= lo) & (lane < lo + EXPERTS_PER_GROUP), logits, ninf)
    v1 = jnp.max(el, axis=1, keepdims=True)
    i1 = jnp.min(jnp.where(el == v1, lane, LANES), axis=1, keepdims=True)
    el2 = jnp.where(lane == i1, ninf, el)
    v2 = jnp.max(el2, axis=1, keepdims=True)
    i2 = jnp.min(jnp.where(el2 == v2, lane, LANES), axis=1, keepdims=True)
    t = jnp.exp(v2 - v1)
    w1 = g_p / (1.0 + t)
    w2 = g_p * t / (1.0 + t)
    e1 = i1 - N_GROUPS
    e2 = i2 - N_GROUPS

    onehot = ((lane == e1) | (lane == e2)).astype(BF16)
    r_i = lax.broadcasted_iota(jnp.int32, (tm, tm), 0)
    c_i = lax.broadcasted_iota(jnp.int32, (tm, tm), 1)
    lower = (c_i < r_i).astype(BF16)
    before = jnp.dot(lower, onehot, preferred_element_type=F32) + carry_ref[...]
    rank1 = jnp.sum(jnp.where(lane == e1, before, 0.0), axis=1, keepdims=True)
    rank2 = jnp.sum(jnp.where(lane == e2, before, 0.0), axis=1, keepdims=True)
    carry_ref[...] = carry_ref[...] + jnp.sum(onehot.astype(F32), axis=0, keepdims=True)
    cnt_ref[...] = carry_ref[...]

    meta = jnp.where(lane == 0, e1.astype(F32),
           jnp.where(lane == 1, e2.astype(F32),
           jnp.where(lane == 2, w1,
           jnp.where(lane == 3, w2,
           jnp.where(lane == 4, rank1,
           jnp.where(lane == 5, rank2, 0.0))))))
    meta_ref[...] = meta


def _proj_norm_route(xres, a, w_bf16, pn_g, pn_b, ln_g, ln_b, wr_hi, wr_lo, br, *, pre_norm):
    tm = TM_ROUTE
    t, d = xres.shape
    row = lambda i: (i, 0)
    fix = lambda i: (0, 0)
    vec = lambda v: v.reshape(1, d)
    kern = functools.partial(_route_kernel, pre_norm=pre_norm)
    return pl.pallas_call(
        kern,
        out_shape=(jax.ShapeDtypeStruct((t, d), F32),
                   jax.ShapeDtypeStruct((t, HALF), jnp.uint32),
                   jax.ShapeDtypeStruct((t, LANES), F32),
                   jax.ShapeDtypeStruct((1, LANES), F32)),
        grid=(t // tm,),
        in_specs=[pl.BlockSpec((tm, d), row),
                  pl.BlockSpec((tm, d), row),
                  pl.BlockSpec((d, d), fix),
                  pl.BlockSpec((1, d), fix), pl.BlockSpec((1, d), fix),
                  pl.BlockSpec((1, d), fix), pl.BlockSpec((1, d), fix),
                  pl.BlockSpec((d, LANES), fix), pl.BlockSpec((d, LANES), fix),
                  pl.BlockSpec((1, LANES), fix)],
        out_specs=(pl.BlockSpec((tm, d), row),
                   pl.BlockSpec((tm, HALF), row),
                   pl.BlockSpec((tm, LANES), row),
                   pl.BlockSpec((1, LANES), fix)),
        scratch_shapes=[pltpu.VMEM((1, LANES), F32)],
        compiler_params=_cparams(("arbitrary",)),
        name="proj_norm_route",
    )(xres, a, w_bf16, vec(pn_g), vec(pn_b), vec(ln_g), vec(ln_b), wr_hi, wr_lo, br)


def _row_copy(src_ref, src_row, dst_ref, dst_row, sem):
    return pltpu.make_async_copy(src_ref.at[pl.ds(src_row, 1), :], dst_ref.at[pl.ds(dst_row, 1), :], sem)


def _scatter_kernel(pos_ref, x_ref, xs_in_ref, xs_ref, sem):
    del xs_in_ref
    tm = x_ref.shape[0]
    base = 2 * pl.program_id(0) * tm

    def start(tok, carry):
        _row_copy(x_ref, tok, xs_ref, pos_ref[base + 2 * tok], sem).start()
        _row_copy(x_ref, tok, xs_ref, pos_ref[base + 2 * tok + 1], sem).start()
        return carry

    def wait(tok, carry):
        _row_copy(x_ref, tok, xs_ref, pos_ref[base + 2 * tok], sem).wait()
        _row_copy(x_ref, tok, xs_ref, pos_ref[base + 2 * tok + 1], sem).wait()
        return carry

    lax.fori_loop(0, tm, start, 0, unroll=DMA_LOOP_UNROLL)
    lax.fori_loop(0, tm, wait, 0, unroll=DMA_LOOP_UNROLL)


def _scatter_rows(pos, xp):
    tm = TM_MOVE
    t, d = xp.shape
    xs0 = jnp.zeros((N_SORTED_ROWS, d), xp.dtype)
    return pl.pallas_call(
        _scatter_kernel,
        out_shape=jax.ShapeDtypeStruct((N_SORTED_ROWS, d), xp.dtype),
        grid_spec=pltpu.PrefetchScalarGridSpec(
            num_scalar_prefetch=1,
            grid=(t // tm,),
            in_specs=[pl.BlockSpec((tm, d), lambda i, *_: (i, 0)),
                      pl.BlockSpec(memory_space=pl.ANY)],
            out_specs=pl.BlockSpec(memory_space=pl.ANY),
            scratch_shapes=[pltpu.SemaphoreType.DMA]),
        input_output_aliases={2: 0},
        compiler_params=_cparams(("arbitrary",)),
        name="moe_scatter",
    )(pos, xp, xs0)


def _expert_kernel(te_ref, nu_ref, xs_ref, wg_ref, wu_ref, wd_ref, y_ref, wg_bf, wu_bf, wd_bf):
    i = pl.program_id(0)
    used = i < nu_ref[0]
    new_expert = (i == 0) | (te_ref[i] != te_ref[jnp.maximum(i - 1, 0)])

    @pl.when(used & new_expert)
    def _():
        wg_bf[...] = wg_ref[...].astype(BF16)
        wu_bf[...] = wu_ref[...].astype(BF16)
        wd_bf[...] = wd_ref[...].astype(BF16)

    @pl.when(used)
    def _():
        x_lo, x_hi = _unpack_bf16_pair(xs_ref[...])
        x_lo = x_lo.astype(BF16)
        x_hi = x_hi.astype(BF16)
        lo_rows = pl.ds(0, HALF)
        hi_rows = pl.ds(HALF, HALF)
        g = (jnp.dot(x_lo, wg_bf[lo_rows, :], preferred_element_type=F32)
             + jnp.dot(x_hi, wg_bf[hi_rows, :], preferred_element_type=F32))
        u = (jnp.dot(x_lo, wu_bf[lo_rows, :], preferred_element_type=F32)
             + jnp.dot(x_hi, wu_bf[hi_rows, :], preferred_element_type=F32))
        h = (jax.nn.silu(g) * u).astype(BF16)
        y = jnp.dot(h, wd_bf[...], preferred_element_type=F32)
        y_ref[...] = _pack_bf16_pair(y[:, :HALF], y[:, HALF:])

    @pl.when(jnp.logical_not(used))
    def _():
        y_ref[...] = jnp.zeros_like(y_ref)


def _expert_mlp(tile_expert, n_used, xs, wg, wu, wd):
    tm = TM_EXPERT
    d, f = D_MODEL, D_EXPERT
    return pl.pallas_call(
        _expert_kernel,
        out_shape=jax.ShapeDtypeStruct((N_SORTED_ROWS, HALF), jnp.uint32),
        grid_spec=pltpu.PrefetchScalarGridSpec(
            num_scalar_prefetch=2,
            grid=(N_EXPERT_TILES,),
            in_specs=[pl.BlockSpec((tm, HALF), lambda i, te, nu: (i, 0)),
                      pl.BlockSpec((None, d, f), lambda i, te, nu: (te[i], 0, 0)),
                      pl.BlockSpec((None, d, f), lambda i, te, nu: (te[i], 0, 0)),
                      pl.BlockSpec((None, f, d), lambda i, te, nu: (te[i], 0, 0))],
            out_specs=pl.BlockSpec((tm, HALF), lambda i, te, nu: (i, 0)),
            scratch_shapes=[pltpu.VMEM((d, f), BF16), pltpu.VMEM((d, f), BF16), pltpu.VMEM((f, d), BF16)]),
        compiler_params=_cparams(("arbitrary",)),
        name="moe_experts",
    )(tile_expert, n_used, xs, wg, wu, wd)


def _combine_kernel(pos_ref, x_ref, meta_ref, y_ref, lng_ref, lnb_ref, o_ref, ybuf_ref, sems):
    tm = x_ref.shape[0]
    i = pl.program_id(0)
    n_steps = pl.num_programs(0)

    def row_copies(step, slot, tok):
        base = 2 * step * tm
        return (_row_copy(y_ref, pos_ref[base + 2 * tok], ybuf_ref.at[slot, 0], tok, sems.at[slot]),
                _row_copy(y_ref, pos_ref[base + 2 * tok + 1], ybuf_ref.at[slot, 1], tok, sems.at[slot]))

    def request(step, slot):
        def body(tok, carry):
            for cp in row_copies(step, slot, tok):
                cp.start()
            return carry
        lax.fori_loop(0, tm, body, 0, unroll=DMA_LOOP_UNROLL)

    def wait(step, slot):
        def body(tok, carry):
            for cp in row_copies(step, slot, tok):
                cp.wait()
            return carry
        lax.fori_loop(0, tm, body, 0, unroll=DMA_LOOP_UNROLL)

    def finish(slot):
        @pl.when(i + 1 < n_steps)
        def _():
            request(i + 1, 1 - slot)

        wait(i, slot)
        meta = meta_ref[...]
        lane = lax.broadcasted_iota(jnp.int32, meta.shape, 1)
        w1 = jnp.sum(jnp.where(lane == 2, meta, 0.0), axis=1, keepdims=True)
        w2 = jnp.sum(jnp.where(lane == 3, meta, 0.0), axis=1, keepdims=True)
        y1_lo, y1_hi = _unpack_bf16_pair(ybuf_ref[slot, 0])
        y2_lo, y2_hi = _unpack_bf16_pair(ybuf_ref[slot, 1])
        y = jnp.concatenate([w1 * y1_lo + w2 * y2_lo, w1 * y1_hi + w2 * y2_hi], axis=1)
        o_ref[...] = _layer_norm(DEEPNORM_ALPHA * x_ref[...] + y, lng_ref[...], lnb_ref[...])

    @pl.when(i == 0)
    def _():
        request(0, 0)

    for slot in range(2):
        pl.when(lax.rem(i, 2) == slot)(functools.partial(finish, slot))


def _combine_norm(pos, x1, meta, y, ln_g, ln_b):
    tm = TM_MOVE
    t, d = x1.shape
    return pl.pallas_call(
        _combine_kernel,
        out_shape=jax.ShapeDtypeStruct((t, d), F32),
        grid_spec=pltpu.PrefetchScalarGridSpec(
            num_scalar_prefetch=1,
            grid=(t // tm,),
            in_specs=[pl.BlockSpec((tm, d), lambda i, *_: (i, 0)),
                      pl.BlockSpec((tm, LANES), lambda i, *_: (i, 0)),
                      pl.BlockSpec(memory_space=pl.ANY),
                      pl.BlockSpec((1, d), lambda i, *_: (0, 0)),
                      pl.BlockSpec((1, d), lambda i, *_: (0, 0))],
            out_specs=pl.BlockSpec((tm, d), lambda i, *_: (i, 0)),
            scratch_shapes=[pltpu.VMEM((2, 2, tm, HALF), jnp.uint32), pltpu.SemaphoreType.DMA((2,))]),
        compiler_params=_cparams(("arbitrary",)),
        name="moe_combine_norm",
    )(pos, x1, meta, y, ln_g.reshape(1, d), ln_b.reshape(1, d))


def _moe_layer(layer, x1, xp, meta, counts, wg, wu, wd, ln_g, ln_b):
    t = x1.shape[0]
    e12 = meta[:, 0:2].astype(jnp.int32).reshape(2 * t)
    r12 = meta[:, 4:6].astype(jnp.int32).reshape(2 * t)
    cnt = counts[0, :N_EXPERTS].astype(jnp.int32)
    padded = ((cnt + TM_EXPERT - 1) // TM_EXPERT) * TM_EXPERT
    ends = jnp.cumsum(padded)
    off = (ends - padded).astype(jnp.int32)
    pos = off[e12] + r12
    tile_start = jnp.arange(N_EXPERT_TILES, dtype=jnp.int32) * TM_EXPERT
    tile_expert = jnp.sum((tile_start[:, None] >= ends[None, :]).astype(jnp.int32), axis=1)
    tile_expert = jnp.minimum(tile_expert, N_EXPERTS - 1) + layer * N_EXPERTS
    n_used = (ends[-1:] // TM_EXPERT).astype(jnp.int32)
    xs = _scatter_rows(pos, xp)
    y = _expert_mlp(tile_expert, n_used, xs, wg, wu, wd)
    return _combine_norm(pos, x1, meta, y, ln_g, ln_b)


def _router_weights(w_grp, b_grp, w_rt, b_rt):
    d = w_grp.shape[0]
    pad = LANES - N_GROUPS - N_EXPERTS
    wr = jnp.concatenate([w_grp, w_rt.reshape(d, N_EXPERTS), jnp.zeros((d, pad), F32)], axis=1)
    br = jnp.concatenate([b_grp, b_rt.reshape(N_EXPERTS), jnp.zeros((pad,), F32)]).reshape(1, LANES)
    wr_hi = _round_to_bf16(wr)
    wr_lo = _round_to_bf16(wr - wr_hi)
    return wr_hi.astype(BF16), wr_lo.astype(BF16), br


def kernel(x, attn_w_qkv, attn_w_o, conv_w_pw1, conv_b_pw1, conv_w_dw, conv_b_dw, conv_ln_g, conv_ln_b,
           conv_w_pw2, mix_ln_g, mix_ln_b, moe_w_grp, moe_b_grp, moe_w_rt, moe_b_rt, moe_w_gate, moe_w_up,
           moe_w_down, ffn_ln_g, ffn_ln_b):
    b, s, d = x.shape
    assert (b, s, d) == (1, SEQ, D_MODEL)
    xt = x.reshape(s, d)
    ones = jnp.ones((d,), F32)
    zeros = jnp.zeros((d,), F32)

    n_all = DEPTH * N_EXPERTS
    moe_w = (moe_w_gate.reshape(n_all, D_MODEL, D_EXPERT), moe_w_up.reshape(n_all, D_MODEL, D_EXPERT),
             moe_w_down.reshape(n_all, D_EXPERT, D_MODEL))

    qkv = _qkv_proj(xt, attn_w_qkv[0].astype(BF16))
    o = _moba_attention(qkv, _moba_key_table(), _moba_slope_table())
    wr_hi, wr_lo, br = _router_weights(moe_w_grp[0], moe_b_grp[0], moe_w_rt[0], moe_b_rt[0])
    x1, xp, meta, counts = _proj_norm_route(xt, o, attn_w_o[0].astype(BF16), ones, zeros, mix_ln_g[0],
                                            mix_ln_b[0], wr_hi, wr_lo, br, pre_norm=False)
    x2 = _moe_layer(0, x1, xp, meta, counts, *moe_w, ffn_ln_g[0], ffn_ln_b[0])

    g = _pw1_glu(x2, conv_w_pw1[0].astype(BF16), conv_b_pw1[0])
    c = _dwconv(g, conv_w_dw[0], conv_b_dw[0])
    wr_hi, wr_lo, br = _router_weights(moe_w_grp[1], moe_b_grp[1], moe_w_rt[1], moe_b_rt[1])
    x3, xp, meta, counts = _proj_norm_route(x2, c, conv_w_pw2[0].astype(BF16), conv_ln_g[0], conv_ln_b[0],
                                            mix_ln_g[1], mix_ln_b[1], wr_hi, wr_lo, br, pre_norm=True)
    x4 = _moe_layer(1, x3, xp, meta, counts, *moe_w, ffn_ln_g[1], ffn_ln_b[1])
    return x4.reshape(b, s, d)
```

```python
import functools

import jax
import jax.numpy as jnp
import numpy as np
from jax import lax
from jax.experimental import pallas as pl
from jax.experimental.pallas import tpu as pltpu

F32 = jnp.float32
BF16 = jnp.bfloat16

D_MODEL = 2048
SEQ = 16384
DEPTH = 2
N_HEADS = 16
HEAD_DIM = D_MODEL // N_HEADS
MOBA_BLOCK = 256
MOBA_TOPK = 3
N_BLOCKS = SEQ // MOBA_BLOCK
CONV_WIDTH = 31
N_GROUPS = 4
EXPERTS_PER_GROUP = 8
N_EXPERTS = N_GROUPS * EXPERTS_PER_GROUP
D_EXPERT = D_MODEL // 4
LN_EPS = 1e-5
NEG_INF = -1e30
DEEPNORM_ALPHA = (2.0 * DEPTH) ** 0.25

LANES = 128
SUBLANES = 8
CONV_HALO = 32
TM_ROUTE = 512
TM_MOVE = 256
DMA_LOOP_UNROLL = 8
TM_EXPERT = 256
N_EXPERT_TILES = (2 * SEQ) // TM_EXPERT + N_EXPERTS
N_SORTED_ROWS = N_EXPERT_TILES * TM_EXPERT
VMEM_LIMIT = 56 * 1024 * 1024


def _round_to_bf16(v):
    return lax.reduce_precision(v, exponent_bits=8, mantissa_bits=7)


def _cparams(sem):
    return pltpu.CompilerParams(dimension_semantics=sem, vmem_limit_bytes=VMEM_LIMIT)


HALF = D_MODEL // 2
_HI16 = 0xFFFF0000


def _pack_bf16_pair(lo, hi):
    lo_b = lax.bitcast_convert_type(lo.astype(BF16).astype(F32), jnp.uint32)
    hi_b = lax.bitcast_convert_type(hi.astype(BF16).astype(F32), jnp.uint32)
    return (hi_b & jnp.uint32(_HI16)) | (lo_b >> 16)


def _unpack_bf16_pair(u):
    lo = lax.bitcast_convert_type(u << 16, F32)
    hi = lax.bitcast_convert_type(u & jnp.uint32(_HI16), F32)
    return lo, hi


def _qkv_kernel(x_ref, w_ref, o_ref, *, n_scaled_tiles, scale):
    acc = jnp.dot(x_ref[...].astype(BF16), w_ref[...], preferred_element_type=F32)
    s = jnp.where(pl.program_id(1) < n_scaled_tiles, scale, 1.0).astype(F32)
    o_ref[...] = (acc * s).astype(o_ref.dtype)


def _qkv_proj(x, w_bf16):
    tm, tn = 1024, 1024
    t, d = x.shape
    n = w_bf16.shape[1]
    kern = functools.partial(_qkv_kernel, n_scaled_tiles=D_MODEL // tn, scale=HEAD_DIM ** -0.5)
    return pl.pallas_call(
        kern,
        out_shape=jax.ShapeDtypeStruct((t, n), BF16),
        grid=(t // tm, n // tn),
        in_specs=[pl.BlockSpec((tm, d), lambda i, j: (i, 0)),
                  pl.BlockSpec((d, tn), lambda i, j: (0, j))],
        out_specs=pl.BlockSpec((tm, tn), lambda i, j: (i, j)),
        compiler_params=_cparams(("parallel", "arbitrary")),
        name="qkv_proj",
    )(x, w_bf16)


def _glu_kernel(x_ref, wa_ref, wb_ref, ba_ref, bb_ref, o_ref):
    xb = x_ref[...].astype(BF16)
    a = jnp.dot(xb, wa_ref[...], preferred_element_type=F32) + ba_ref[...]
    b = jnp.dot(xb, wb_ref[...], preferred_element_type=F32) + bb_ref[...]
    o_ref[...] = a * jax.nn.sigmoid(b)


def _pw1_glu(x, w_bf16, bias):
    tm, tn = 1024, 512
    t, d = x.shape
    nb = d // tn
    bias2 = bias.reshape(1, 2 * d)
    return pl.pallas_call(
        _glu_kernel,
        out_shape=jax.ShapeDtypeStruct((t, d), F32),
        grid=(t // tm, nb),
        in_specs=[pl.BlockSpec((tm, d), lambda i, j: (i, 0)),
                  pl.BlockSpec((d, tn), lambda i, j: (0, j)),
                  pl.BlockSpec((d, tn), lambda i, j: (0, j + nb)),
                  pl.BlockSpec((1, tn), lambda i, j: (0, j)),
                  pl.BlockSpec((1, tn), lambda i, j: (0, j + nb))],
        out_specs=pl.BlockSpec((tm, tn), lambda i, j: (i, j)),
        compiler_params=_cparams(("parallel", "arbitrary")),
        name="pw1_glu",
    )(x, w_bf16, w_bf16, bias2, bias2)


N_SLOPE_SPLIT = 3
KPOS_LO = 128
HEADS_PER_STEP = 2
CHUNK_BLOCKS = 4
BIAS_LANE0 = N_BLOCKS
N_BIAS_LANES = 2 * N_SLOPE_SPLIT


def _moba_key_table():
    kpos = np.arange(SEQ)
    tab = np.zeros((SEQ, LANES), np.float32)
    tab[kpos, kpos // MOBA_BLOCK] = 1.0
    tab[:, BIAS_LANE0:BIAS_LANE0 + N_SLOPE_SPLIT] = ((kpos // KPOS_LO) * KPOS_LO)[:, None]
    tab[:, BIAS_LANE0 + N_SLOPE_SPLIT:BIAS_LANE0 + N_BIAS_LANES] = (kpos % KPOS_LO)[:, None]
    return jnp.asarray(tab.astype(BF16))


def _moba_slope_table():
    slopes = 2.0 ** (-8.0 * jnp.arange(1, N_HEADS + 1, dtype=F32) / N_HEADS)
    pieces = []
    rem = slopes
    for _ in range(N_SLOPE_SPLIT):
        p = _round_to_bf16(rem)
        pieces.append(p[:, None])
        rem = rem - p
    tab = jnp.concatenate([jnp.zeros((N_HEADS, BIAS_LANE0), F32)] + pieces + pieces
                          + [jnp.zeros((N_HEADS, LANES - BIAS_LANE0 - N_BIAS_LANES), F32)], axis=1)
    tab = jnp.broadcast_to(tab[:, :, None], (N_HEADS, LANES, MOBA_BLOCK))
    return tab.reshape(N_HEADS // HEADS_PER_STEP, HEADS_PER_STEP * LANES, MOBA_BLOCK)


def _moba_kernel(qt_ref, k_ref, vt_ref, c_ref, sl_ref, o_ref, kmean_ref, qa_ref, m_ref, l_ref, acc_ref,
                 s0_ref, s1_ref):
    blk = MOBA_BLOCK
    hp_n = HEADS_PER_STEP
    own = pl.program_id(1)
    hcols = lambda hp: slice(hp * HEAD_DIM, (hp + 1) * HEAD_DIM)

    @pl.when(own == 0)
    def _():
        kmean_ref[...] = jnp.zeros_like(kmean_ref)

        def body(j, carry):
            kb = k_ref[pl.ds(pl.multiple_of(j * blk, blk), blk), :].astype(F32)
            mean = jnp.mean(kb, axis=0, keepdims=True)
            for hp in range(hp_n):
                kmean_ref[hp, pl.ds(j, 1), :] = mean[:, hcols(hp)]
            return carry

        lax.fori_loop(0, N_BLOCKS, body, 0)

    row = lax.broadcasted_iota(jnp.int32, (LANES, blk), 0)
    minus_inf = jnp.float32(-jnp.inf)
    own_rows = pl.ds(pl.multiple_of(own * blk, blk), blk)
    key_i = lax.broadcasted_iota(jnp.int32, (blk, blk), 0)
    qry_i = lax.broadcasted_iota(jnp.int32, (blk, blk), 1)
    hrows = hcols
    for hp in range(hp_n):
        qt = qt_ref[hrows(hp), :]
        km = kmean_ref[hp]
        km_hi = km.astype(BF16)
        km_lo = (km - km_hi.astype(F32)).astype(BF16)
        gate = (jnp.dot(km_hi, qt, preferred_element_type=F32)
                + jnp.dot(km_lo, qt, preferred_element_type=F32))
        gm = jnp.where(row < own, gate, minus_inf)
        sel = row < 0
        for r in range(MOBA_TOPK):
            mx = jnp.max(gm, axis=0, keepdims=True)
            idx = jnp.min(jnp.where(gm == mx, row, LANES), axis=0, keepdims=True)
            hit = row == idx
            sel = sel | (hit & (r < own))
            gm = jnp.where(hit, minus_inf, gm)
        slope_rows = sl_ref[pl.ds(hp * LANES, LANES), :]
        aug = jnp.where(row < N_BLOCKS, jnp.where(sel, 0.0, NEG_INF), slope_rows).astype(BF16)
        qa_ref[hp] = jnp.concatenate([qt, aug], axis=0)

        aug_own = jnp.where(row < N_BLOCKS, 0.0, slope_rows).astype(BF16)
        k_aug = jnp.concatenate([k_ref[own_rows, hcols(hp)], c_ref[own_rows, :]], axis=1)
        s = jnp.dot(k_aug, jnp.concatenate([qt, aug_own], axis=0), preferred_element_type=F32)
        s = jnp.where(key_i <= qry_i, s, NEG_INF)
        m0 = jnp.max(s, axis=0, keepdims=True)
        p = jnp.exp(s - m0)
        m_ref[hp] = m0
        l_ref[hp] = jnp.sum(p, axis=0, keepdims=True)
        acc_ref[hp] = jnp.dot(vt_ref[own, hrows(hp), :], p.astype(BF16), preferred_element_type=F32)

    width = CHUNK_BLOCKS * blk
    last_chunk = SEQ // width - 1

    def score_chunk(c, s_ref):
        rows = pl.ds(pl.multiple_of(c * width, width), width)
        c_rows = c_ref[rows, :]
        for hp in range(hp_n):
            k_aug = jnp.concatenate([k_ref[rows, hcols(hp)], c_rows], axis=1)
            s_ref[hp] = jnp.dot(k_aug, qa_ref[hp], preferred_element_type=F32)

    def absorb_chunk(c, s_ref):
        for hp in range(hp_n):
            s = s_ref[hp]
            m_old = m_ref[hp]
            m_new = jnp.maximum(m_old, jnp.max(s, axis=0, keepdims=True))
            alpha = jnp.exp(m_old - m_new)
            p = jnp.exp(s - m_new)
            l_ref[hp] = alpha * l_ref[hp] + jnp.sum(p, axis=0, keepdims=True)
            p = p.astype(BF16)
            pv = jnp.dot(vt_ref[c * CHUNK_BLOCKS, hrows(hp), :], p[:blk], preferred_element_type=F32)
            for b in range(1, CHUNK_BLOCKS):
                pv = pv + jnp.dot(vt_ref[c * CHUNK_BLOCKS + b, hrows(hp), :], p[b * blk:(b + 1) * blk],
                                  preferred_element_type=F32)
            acc_ref[hp] = alpha * acc_ref[hp] + pv
            m_ref[hp] = m_new

    score_chunk(0, s0_ref)

    def body(i, carry):
        c0 = 2 * i
        score_chunk(c0 + 1, s1_ref)
        absorb_chunk(c0, s0_ref)
        score_chunk(jnp.minimum(c0 + 2, last_chunk), s0_ref)
        absorb_chunk(c0 + 1, s1_ref)
        return carry

    n_chunks = lax.div(own + (CHUNK_BLOCKS - 1), jnp.int32(CHUNK_BLOCKS))
    n_pairs = lax.div(n_chunks, jnp.int32(2))
    lax.fori_loop(0, n_pairs, body, 0)

    @pl.when(n_chunks > 2 * n_pairs)
    def _():
        absorb_chunk(2 * n_pairs, s0_ref)
    for hp in range(hp_n):
        o_ref[:, hcols(hp)] = (acc_ref[hp] / l_ref[hp]).T.astype(o_ref.dtype)


def _moba_attention(qkv, key_table, slope_table):
    blk = MOBA_BLOCK
    hp_n = HEADS_PER_STEP
    w = hp_n * HEAD_DIM
    n_groups = N_HEADS // hp_n
    q_t = qkv[:, :D_MODEL].reshape(N_BLOCKS, blk, D_MODEL).transpose(0, 2, 1)
    v_t = qkv[:, 2 * D_MODEL:].reshape(N_BLOCKS, blk, D_MODEL).transpose(0, 2, 1)
    return pl.pallas_call(
        _moba_kernel,
        out_shape=jax.ShapeDtypeStruct((SEQ, D_MODEL), BF16),
        grid=(n_groups, N_BLOCKS),
        in_specs=[pl.BlockSpec((None, w, blk), lambda h, i: (i, h, 0)),
                  pl.BlockSpec((SEQ, w), lambda h, i: (0, n_groups + h)),
                  pl.BlockSpec((N_BLOCKS, w, blk), lambda h, i: (0, h, 0)),
                  pl.BlockSpec((SEQ, LANES), lambda h, i: (0, 0)),
                  pl.BlockSpec((None, hp_n * LANES, blk), lambda h, i: (h, 0, 0))],
        out_specs=pl.BlockSpec((blk, w), lambda h, i: (i, h)),
        scratch_shapes=[pltpu.VMEM((hp_n, LANES, HEAD_DIM), F32),
                        pltpu.VMEM((hp_n, HEAD_DIM + LANES, blk), BF16),
                        pltpu.VMEM((hp_n, 1, blk), F32),
                        pltpu.VMEM((hp_n, 1, blk), F32),
                        pltpu.VMEM((hp_n, HEAD_DIM, blk), F32),
                        pltpu.VMEM((hp_n, CHUNK_BLOCKS * blk, blk), F32),
                        pltpu.VMEM((hp_n, CHUNK_BLOCKS * blk, blk), F32)],
        compiler_params=_cparams(("arbitrary", "arbitrary")),
        name="moba_attention",
    )(q_t, qkv, v_t, key_table, slope_table)


def _dwconv_kernel(cur_ref, prev_ref, w_ref, b_ref, o_ref, win_ref, sh_ref, *, row_chunk):
    tm = cur_ref.shape[0]
    keep = (pl.program_id(0) > 0).astype(F32)
    win_ref[pl.ds(0, CONV_HALO), :] = prev_ref[...] * keep
    win_ref[pl.ds(CONV_HALO, tm), :] = cur_ref[...]
    first = CONV_HALO - (CONV_WIDTH - 1)
    n_sh = sh_ref.shape[1]
    for b in range(1, SUBLANES):
        sh_ref[b - 1] = win_ref[pl.ds(b, n_sh), :]
    for rc in range(tm // row_chunk):
        acc = jnp.broadcast_to(b_ref[...], (row_chunk, cur_ref.shape[1]))
        for k in range(CONV_WIDTH):
            b = (first + k) % SUBLANES
            a = first + k - b + rc * row_chunk
            src = win_ref if b == 0 else sh_ref.at[b - 1]
            acc = acc + w_ref[pl.ds(k, 1), :] * src[pl.ds(a, row_chunk), :]
        o_ref[pl.ds(rc * row_chunk, row_chunk), :] = acc


def _dwconv(g, w_dw, b_dw):
    tm, tc = 512, 512
    t, d = g.shape
    w_pad = jnp.concatenate([w_dw, jnp.zeros((CONV_HALO - CONV_WIDTH, d), F32)], axis=0)
    halo_per_tile = tm // CONV_HALO
    kern = functools.partial(_dwconv_kernel, row_chunk=64)
    return pl.pallas_call(
        kern,
        out_shape=jax.ShapeDtypeStruct((t, d), F32),
        grid=(t // tm, d // tc),
        in_specs=[pl.BlockSpec((tm, tc), lambda i, j: (i, j)),
                  pl.BlockSpec((CONV_HALO, tc), lambda i, j: (jnp.maximum(i * halo_per_tile - 1, 0), j)),
                  pl.BlockSpec((CONV_HALO, tc), lambda i, j: (0, j)),
                  pl.BlockSpec((1, tc), lambda i, j: (0, j))],
        out_specs=pl.BlockSpec((tm, tc), lambda i, j: (i, j)),
        scratch_shapes=[pltpu.VMEM((tm + CONV_HALO, tc), F32),
                        pltpu.VMEM((SUBLANES - 1, tm + CONV_HALO - SUBLANES, tc), F32)],
        compiler_params=_cparams(("parallel", "arbitrary")),
        name="dwconv",
    )(g, g, w_pad, b_dw.reshape(1, d))


def _layer_norm(z, g, b):
    mu = jnp.mean(z, axis=-1, keepdims=True)
    zc = z - mu
    var = jnp.mean(zc * zc, axis=-1, keepdims=True)
    return zc * lax.rsqrt(var + LN_EPS) * g + b


def _route_kernel(xres_ref, a_ref, w_ref, png_ref, pnb_ref, lng_ref, lnb_ref,
                  wr_hi_ref, wr_lo_ref, br_ref,
                  x1_ref, xp_ref, meta_ref, cnt_ref, carry_ref, *, pre_norm):
    tm = xres_ref.shape[0]

    @pl.when(pl.program_id(0) == 0)
    def _():
        carry_ref[...] = jnp.zeros_like(carry_ref)

    a = a_ref[...]
    if pre_norm:
        a = jax.nn.silu(_layer_norm(a.astype(F32), png_ref[...], pnb_ref[...]))
    y = jnp.dot(a.astype(BF16), w_ref[...], preferred_element_type=F32)
    x1 = _layer_norm(DEEPNORM_ALPHA * xres_ref[...] + y, lng_ref[...], lnb_ref[...])
    x1_ref[...] = x1
    xp_ref[...] = _pack_bf16_pair(x1[:, :HALF], x1[:, HALF:])

    x_hi = x1.astype(BF16)
    x_lo = (x1 - x_hi.astype(F32)).astype(BF16)
    logits = (jnp.dot(x_hi, wr_hi_ref[...], preferred_element_type=F32)
              + jnp.dot(x_lo, wr_hi_ref[...], preferred_element_type=F32)
              + jnp.dot(x_hi, wr_lo_ref[...], preferred_element_type=F32)
              + br_ref[...])
    lane = lax.broadcasted_iota(jnp.int32, logits.shape, 1)
    ninf = jnp.float32(-jnp.inf)
    gl = jnp.where(lane < N_GROUPS, logits, ninf)
    gmax = jnp.max(gl, axis=1, keepdims=True)
    g_sel = jnp.min(jnp.where(gl == gmax, lane, LANES), axis=1, keepdims=True)
    g_p = 1.0 / jnp.sum(jnp.exp(gl - gmax), axis=1, keepdims=True)
    lo = N_GROUPS + EXPERTS_PER_GROUP * g_sel
    el = jnp.where((lane >= lo) & (lane < lo + EXPERTS_PER_GROUP), logits, ninf)
    v1 = jnp.max(el, axis=1, keepdims=True)
    i1 = jnp.min(jnp.where(el == v1, lane, LANES), axis=1, keepdims=True)
    el2 = jnp.where(lane == i1, ninf, el)
    v2 = jnp.max(el2, axis=1, keepdims=True)
    i2 = jnp.min(jnp.where(el2 == v2, lane, LANES), axis=1, keepdims=True)
    t = jnp.exp(v2 - v1)
    w1 = g_p / (1.0 + t)
    w2 = g_p * t / (1.0 + t)
    e1 = i1 - N_GROUPS
    e2 = i2 - N_GROUPS

    onehot = ((lane == e1) | (lane == e2)).astype(BF16)
    r_i = lax.broadcasted_iota(jnp.int32, (tm, tm), 0)
    c_i = lax.broadcasted_iota(jnp.int32, (tm, tm), 1)
    lower = (c_i < r_i).astype(BF16)
    before = jnp.dot(lower, onehot, preferred_element_type=F32) + carry_ref[...]
    rank1 = jnp.sum(jnp.where(lane == e1, before, 0.0), axis=1, keepdims=True)
    rank2 = jnp.sum(jnp.where(lane == e2, before, 0.0), axis=1, keepdims=True)
    carry_ref[...] = carry_ref[...] + jnp.sum(onehot.astype(F32), axis=0, keepdims=True)
    cnt_ref[...] = carry_ref[...]

    meta = jnp.where(lane == 0, e1.astype(F32),
           jnp.where(lane == 1, e2.astype(F32),
           jnp.where(lane == 2, w1,
           jnp.where(lane == 3, w2,
           jnp.where(lane == 4, rank1,
           jnp.where(lane == 5, rank2, 0.0))))))
    meta_ref[...] = meta


def _proj_norm_route(xres, a, w_bf16, pn_g, pn_b, ln_g, ln_b, wr_hi, wr_lo, br, *, pre_norm):
    tm = TM_ROUTE
    t, d = xres.shape
    row = lambda i: (i, 0)
    fix = lambda i: (0, 0)
    vec = lambda v: v.reshape(1, d)
    kern = functools.partial(_route_kernel, pre_norm=pre_norm)
    return pl.pallas_call(
        kern,
        out_shape=(jax.ShapeDtypeStruct((t, d), F32),
                   jax.ShapeDtypeStruct((t, HALF), jnp.uint32),
                   jax.ShapeDtypeStruct((t, LANES), F32),
                   jax.ShapeDtypeStruct((1, LANES), F32)),
        grid=(t // tm,),
        in_specs=[pl.BlockSpec((tm, d), row),
                  pl.BlockSpec((tm, d), row),
                  pl.BlockSpec((d, d), fix),
                  pl.BlockSpec((1, d), fix), pl.BlockSpec((1, d), fix),
                  pl.BlockSpec((1, d), fix), pl.BlockSpec((1, d), fix),
                  pl.BlockSpec((d, LANES), fix), pl.BlockSpec((d, LANES), fix),
                  pl.BlockSpec((1, LANES), fix)],
        out_specs=(pl.BlockSpec((tm, d), row),
                   pl.BlockSpec((tm, HALF), row),
                   pl.BlockSpec((tm, LANES), row),
                   pl.BlockSpec((1, LANES), fix)),
        scratch_shapes=[pltpu.VMEM((1, LANES), F32)],
        compiler_params=_cparams(("arbitrary",)),
        name="proj_norm_route",
    )(xres, a, w_bf16, vec(pn_g), vec(pn_b), vec(ln_g), vec(ln_b), wr_hi, wr_lo, br)


def _row_copy(src_ref, src_row, dst_ref, dst_row, sem):
    return pltpu.make_async_copy(src_ref.at[pl.ds(src_row, 1), :], dst_ref.at[pl.ds(dst_row, 1), :], sem)


def _scatter_kernel(pos_ref, x_ref, xs_in_ref, xs_ref, sem):
    del xs_in_ref
    tm = x_ref.shape[0]
    base = 2 * pl.program_id(0) * tm

    def start(tok, carry):
        _row_copy(x_ref, tok, xs_ref, pos_ref[base + 2 * tok], sem).start(priority=0)
        _row_copy(x_ref, tok, xs_ref, pos_ref[base + 2 * tok + 1], sem).start(priority=1)
        return carry

    def wait(tok, carry):
        _row_copy(x_ref, tok, xs_ref, pos_ref[base + 2 * tok], sem).wait()
        _row_copy(x_ref, tok, xs_ref, pos_ref[base + 2 * tok + 1], sem).wait()
        return carry

    lax.fori_loop(0, tm, start, 0, unroll=DMA_LOOP_UNROLL)
    lax.fori_loop(0, tm, wait, 0, unroll=DMA_LOOP_UNROLL)


def _scatter_rows(pos, xp):
    tm = TM_MOVE
    t, d = xp.shape
    xs0 = jnp.zeros((N_SORTED_ROWS, d), xp.dtype)
    return pl.pallas_call(
        _scatter_kernel,
        out_shape=jax.ShapeDtypeStruct((N_SORTED_ROWS, d), xp.dtype),
        grid_spec=pltpu.PrefetchScalarGridSpec(
            num_scalar_prefetch=1,
            grid=(t // tm,),
            in_specs=[pl.BlockSpec((tm, d), lambda i, *_: (i, 0)),
                      pl.BlockSpec(memory_space=pl.ANY)],
            out_specs=pl.BlockSpec(memory_space=pl.ANY),
            scratch_shapes=[pltpu.SemaphoreType.DMA]),
        input_output_aliases={2: 0},
        compiler_params=_cparams(("arbitrary",)),
        name="moe_scatter",
    )(pos, xp, xs0)


def _expert_kernel(te_ref, nu_ref, xs_ref, wg_ref, wu_ref, wd_ref, y_ref, wg_bf, wu_bf, wd_bf):
    i = pl.program_id(0)
    used = i < nu_ref[0]
    new_expert = (i == 0) | (te_ref[i] != te_ref[jnp.maximum(i - 1, 0)])

    @pl.when(used & new_expert)
    def _():
        wg_bf[...] = wg_ref[...].astype(BF16)
        wu_bf[...] = wu_ref[...].astype(BF16)
        wd_bf[...] = wd_ref[...].astype(BF16)

    @pl.when(used)
    def _():
        x_lo, x_hi = _unpack_bf16_pair(xs_ref[...])
        x_lo = x_lo.astype(BF16)
        x_hi = x_hi.astype(BF16)
        lo_rows = pl.ds(0, HALF)
        hi_rows = pl.ds(HALF, HALF)
        g = (jnp.dot(x_lo, wg_bf[lo_rows, :], preferred_element_type=F32)
             + jnp.dot(x_hi, wg_bf[hi_rows, :], preferred_element_type=F32))
        u = (jnp.dot(x_lo, wu_bf[lo_rows, :], preferred_element_type=F32)
             + jnp.dot(x_hi, wu_bf[hi_rows, :], preferred_element_type=F32))
        h = (jax.nn.silu(g) * u).astype(BF16)
        y = jnp.dot(h, wd_bf[...], preferred_element_type=F32)
        y_ref[...] = _pack_bf16_pair(y[:, :HALF], y[:, HALF:])

    @pl.when(jnp.logical_not(used))
    def _():
        y_ref[...] = jnp.zeros_like(y_ref)


def _expert_mlp(tile_expert, n_used, xs, wg, wu, wd):
    tm = TM_EXPERT
    d, f = D_MODEL, D_EXPERT
    return pl.pallas_call(
        _expert_kernel,
        out_shape=jax.ShapeDtypeStruct((N_SORTED_ROWS, HALF), jnp.uint32),
        grid_spec=pltpu.PrefetchScalarGridSpec(
            num_scalar_prefetch=2,
            grid=(N_EXPERT_TILES,),
            in_specs=[pl.BlockSpec((tm, HALF), lambda i, te, nu: (i, 0)),
                      pl.BlockSpec((None, d, f), lambda i, te, nu: (te[i], 0, 0)),
                      pl.BlockSpec((None, d, f), lambda i, te, nu: (te[i], 0, 0)),
                      pl.BlockSpec((None, f, d), lambda i, te, nu: (te[i], 0, 0))],
            out_specs=pl.BlockSpec((tm, HALF), lambda i, te, nu: (i, 0)),
            scratch_shapes=[pltpu.VMEM((d, f), BF16), pltpu.VMEM((d, f), BF16), pltpu.VMEM((f, d), BF16)]),
        compiler_params=_cparams(("arbitrary",)),
        name="moe_experts",
    )(tile_expert, n_used, xs, wg, wu, wd)


def _combine_kernel(pos_ref, x_ref, meta_ref, y_ref, lng_ref, lnb_ref, o_ref, ybuf_ref, sems):
    tm = x_ref.shape[0]
    i = pl.program_id(0)
    n_steps = pl.num_programs(0)

    def row_copies(step, slot, tok):
        base = 2 * step * tm
        return (_row_copy(y_ref, pos_ref[base + 2 * tok], ybuf_ref.at[slot, 0], tok, sems.at[slot]),
                _row_copy(y_ref, pos_ref[base + 2 * tok + 1], ybuf_ref.at[slot, 1], tok, sems.at[slot]))

    def request(step, slot):
        def body(tok, carry):
            for priority, cp in enumerate(row_copies(step, slot, tok)):
                cp.start(priority=priority)
            return carry
        lax.fori_loop(0, tm, body, 0, unroll=DMA_LOOP_UNROLL)

    def wait(step, slot):
        def body(tok, carry):
            for cp in row_copies(step, slot, tok):
                cp.wait()
            return carry
        lax.fori_loop(0, tm, body, 0, unroll=DMA_LOOP_UNROLL)

    def finish(slot):
        @pl.when(i + 1 < n_steps)
        def _():
            request(i + 1, 1 - slot)

        wait(i, slot)
        meta = meta_ref[...]
        lane = lax.broadcasted_iota(jnp.int32, meta.shape, 1)
        w1 = jnp.sum(jnp.where(lane == 2, meta, 0.0), axis=1, keepdims=True)
        w2 = jnp.sum(jnp.where(lane == 3, meta, 0.0), axis=1, keepdims=True)
        y1_lo, y1_hi = _unpack_bf16_pair(ybuf_ref[slot, 0])
        y2_lo, y2_hi = _unpack_bf16_pair(ybuf_ref[slot, 1])
        y = jnp.concatenate([w1 * y1_lo + w2 * y2_lo, w1 * y1_hi + w2 * y2_hi], axis=1)
        o_ref[...] = _layer_norm(DEEPNORM_ALPHA * x_ref[...] + y, lng_ref[...], lnb_ref[...])

    @pl.when(i == 0)
    def _():
        request(0, 0)

    for slot in range(2):
        pl.when(lax.rem(i, 2) == slot)(functools.partial(finish, slot))


def _combine_norm(pos, x1, meta, y, ln_g, ln_b):
    tm = TM_MOVE
    t, d = x1.shape
    return pl.pallas_call(
        _combine_kernel,
        out_shape=jax.ShapeDtypeStruct((t, d), F32),
        grid_spec=pltpu.PrefetchScalarGridSpec(
            num_scalar_prefetch=1,
            grid=(t // tm,),
            in_specs=[pl.BlockSpec((tm, d), lambda i, *_: (i, 0)),
                      pl.BlockSpec((tm, LANES), lambda i, *_: (i, 0)),
                      pl.BlockSpec(memory_space=pl.ANY),
                      pl.BlockSpec((1, d), lambda i, *_: (0, 0)),
                      pl.BlockSpec((1, d), lambda i, *_: (0, 0))],
            out_specs=pl.BlockSpec((tm, d), lambda i, *_: (i, 0)),
            scratch_shapes=[pltpu.VMEM((2, 2, tm, HALF), jnp.uint32), pltpu.SemaphoreType.DMA((2,))]),
        compiler_params=_cparams(("arbitrary",)),
        name="moe_combine_norm",
    )(pos, x1, meta, y, ln_g.reshape(1, d), ln_b.reshape(1, d))


def _moe_layer(layer, x1, xp, meta, counts, wg, wu, wd, ln_g, ln_b):
    t = x1.shape[0]
    e12 = meta[:, 0:2].astype(jnp.int32).reshape(2 * t)
    r12 = meta[:, 4:6].astype(jnp.int32).reshape(2 * t)
    cnt = counts[0, :N_EXPERTS].astype(jnp.int32)
    padded = ((cnt + TM_EXPERT - 1) // TM_EXPERT) * TM_EXPERT
    ends = jnp.cumsum(padded)
    off = (ends - padded).astype(jnp.int32)
    pos = off[e12] + r12
    tile_start = jnp.arange(N_EXPERT_TILES, dtype=jnp.int32) * TM_EXPERT
    tile_expert = jnp.sum((tile_start[:, None] >= ends[None, :]).astype(jnp.int32), axis=1)
    tile_expert = jnp.minimum(tile_expert, N_EXPERTS - 1) + layer * N_EXPERTS
    n_used = (ends[-1:] // TM_EXPERT).astype(jnp.int32)
    xs = _scatter_rows(pos, xp)
    y = _expert_mlp(tile_expert, n_used, xs, wg, wu, wd)
    return _combine_norm(pos, x1, meta, y, ln_g, ln_b)


def _router_weights(w_grp, b_grp, w_rt, b_rt):
    d = w_grp.shape[0]
    pad = LANES - N_GROUPS - N_EXPERTS
    wr = jnp.concatenate([w_grp, w_rt.reshape(d, N_EXPERTS), jnp.zeros((d, pad), F32)], axis=1)
    br = jnp.concatenate([b_grp, b_rt.reshape(N_EXPERTS), jnp.zeros((pad,), F32)]).reshape(1, LANES)
    wr_hi = _round_to_bf16(wr)
    wr_lo = _round_to_bf16(wr - wr_hi)
    return wr_hi.astype(BF16), wr_lo.astype(BF16), br


def kernel(x, attn_w_qkv, attn_w_o, conv_w_pw1, conv_b_pw1, conv_w_dw, conv_b_dw, conv_ln_g, conv_ln_b,
           conv_w_pw2, mix_ln_g, mix_ln_b, moe_w_grp, moe_b_grp, moe_w_rt, moe_b_rt, moe_w_gate, moe_w_up,
           moe_w_down, ffn_ln_g, ffn_ln_b):
    b, s, d = x.shape
    assert (b, s, d) == (1, SEQ, D_MODEL)
    xt = x.reshape(s, d)
    ones = jnp.ones((d,), F32)
    zeros = jnp.zeros((d,), F32)

    n_all = DEPTH * N_EXPERTS
    moe_w = (moe_w_gate.reshape(n_all, D_MODEL, D_EXPERT), moe_w_up.reshape(n_all, D_MODEL, D_EXPERT),
             moe_w_down.reshape(n_all, D_EXPERT, D_MODEL))

    qkv = _qkv_proj(xt, attn_w_qkv[0].astype(BF16))
    o = _moba_attention(qkv, _moba_key_table(), _moba_slope_table())
    wr_hi, wr_lo, br = _router_weights(moe_w_grp[0], moe_b_grp[0], moe_w_rt[0], moe_b_rt[0])
    x1, xp, meta, counts = _proj_norm_route(xt, o, attn_w_o[0].astype(BF16), ones, zeros, mix_ln_g[0],
                                            mix_ln_b[0], wr_hi, wr_lo, br, pre_norm=False)
    x2 = _moe_layer(0, x1, xp, meta, counts, *moe_w, ffn_ln_g[0], ffn_ln_b[0])

    g = _pw1_glu(x2, conv_w_pw1[0].astype(BF16), conv_b_pw1[0])
    c = _dwconv(g, conv_w_dw[0], conv_b_dw[0])
    wr_hi, wr_lo, br = _router_weights(moe_w_grp[1], moe_b_grp[1], moe_w_rt[1], moe_b_rt[1])
    x3, xp, meta, counts = _proj_norm_route(x2, c, conv_w_pw2[0].astype(BF16), conv_ln_g[0], conv_ln_b[0],
                                            mix_ln_g[1], mix_ln_b[1], wr_hi, wr_lo, br, pre_norm=True)
    x4 = _moe_layer(1, x3, xp, meta, counts, *moe_w, ffn_ln_g[1], ffn_ln_b[1])
    return x4.reshape(b, s, d)
```
